```python
import jax, jax.numpy as jnp
from jax import lax
import numpy as np

D_MODEL = 2048
BATCH = 2
SEQ = 4096
DEPTH = 2
DEC_BATCH = 2
DEC_SEQ = 16384
PAST_LEN = 128

CONV_DIM = 1024
CONV_WIDTH = 31
MLSTM_HEADS = 4
MLSTM_HEAD_QK = 256
MLSTM_HEAD_V = 512
MLSTM_QK_DIM = MLSTM_HEADS * MLSTM_HEAD_QK
MLSTM_V_DIM = MLSTM_HEADS * MLSTM_HEAD_V
CHUNK = 128
D_FF = 5504
FFN_CONV_WIDTH = 3
EPS = 1e-6
IN_SIZES = (2 * CONV_DIM, MLSTM_QK_DIM, MLSTM_QK_DIM, MLSTM_V_DIM, MLSTM_V_DIM, 4 * MLSTM_HEADS, 2 * D_MODEL)
D_IN = sum(IN_SIZES)
IN_SPLITS = tuple(int(s) for s in np.cumsum(IN_SIZES)[:-1])

kernel_name = 'gated_conformer_mlstm_encoder'


def _rmsnorm(x, g):
    xf = x.astype(jnp.float32)
    y = xf * lax.rsqrt(jnp.mean(xf * xf, -1, keepdims=True) + EPS)
    return (y * g.astype(jnp.float32)).astype(x.dtype)


def _layernorm(x, g, b):
    xf = x.astype(jnp.float32)
    mu = jnp.mean(xf, -1, keepdims=True)
    xc = xf - mu
    y = xc * lax.rsqrt(jnp.mean(xc * xc, -1, keepdims=True) + EPS)
    return (y * g.astype(jnp.float32) + b.astype(jnp.float32)).astype(x.dtype)


def _depthwise_conv(x, w, b):
    width = w.shape[0]
    pad = (width - 1) // 2
    y = lax.conv_general_dilated(x, w[:, None, :].astype(x.dtype), window_strides=(1,),
                                 padding=[(pad, pad)], dimension_numbers=('NWC', 'WIO', 'NWC'),
                                 feature_group_count=x.shape[-1])
    return y + b.astype(x.dtype)


def _mlstm_chunkwise(q, k, v, ig, lf):
    B, H, S, DK = q.shape
    DV = v.shape[-1]
    nc = S // CHUNK

    def to_chunks(t):
        return jnp.moveaxis(t.reshape(B, H, nc, CHUNK, *t.shape[3:]), 2, 0)

    xs = tuple(to_chunks(t) for t in (q, k, v, ig, lf))
    causal = jnp.tril(jnp.ones((CHUNK, CHUNK), bool))

    def step(carry, inp):
        C, n, m = carry
        qb, kb, vb, igb, lfb = inp
        b = jnp.cumsum(lfb, -1)
        dmat = b[..., :, None] - b[..., None, :] + igb[..., None, :]
        dmat = jnp.where(causal, dmat, -jnp.inf)
        a = b + m[..., None]
        m_t = jnp.maximum(a, jnp.max(dmat, -1))
        w = jnp.exp(dmat - m_t[..., None])
        s = jnp.einsum('bhtd,bhsd->bhts', qb, kb) * w
        inter = jnp.exp(a - m_t)
        num = inter[..., None] * jnp.einsum('bhtd,bhde->bhte', qb, C) + jnp.einsum('bhts,bhse->bhte', s, vb)
        den = inter * jnp.einsum('bhtd,bhd->bht', qb, n) + jnp.sum(s, -1)
        h = num / jnp.maximum(jnp.abs(den), jnp.exp(-m_t))[..., None]
        b_last = b[..., -1]
        g = b_last[..., None] - b + igb
        m_new = jnp.maximum(b_last + m, jnp.max(g, -1))
        decay = jnp.exp(b_last + m - m_new)
        wk = jnp.exp(g - m_new[..., None])[..., None] * kb
        C_new = decay[..., None, None] * C + jnp.einsum('bhsd,bhse->bhde', wk, vb)
        n_new = decay[..., None] * n + jnp.sum(wk, -2)
        return (C_new, n_new, m_new), h

    init = (jnp.zeros((B, H, DK, DV), jnp.float32), jnp.zeros((B, H, DK), jnp.float32),
            jnp.zeros((B, H), jnp.float32))
    _, hc = lax.scan(step, init, xs)
    return jnp.moveaxis(hc, 0, 2).reshape(B, H, S, DV)


def _mixer(xn, w_in, b_gates, conv_dw_w, conv_dw_b, conv_ln_g, conv_ln_b, w_conv_out,
           mlstm_head_g, w_mlstm_out, w_out):
    B, S, _ = xn.shape
    proj = xn @ w_in.astype(xn.dtype)
    conv_in, q, k, v, o_pre, gate_pre, merge_pre = jnp.split(proj, IN_SPLITS, axis=-1)

    u = conv_in[..., :CONV_DIM] * jax.nn.sigmoid(conv_in[..., CONV_DIM:])
    u = _depthwise_conv(u, conv_dw_w, conv_dw_b)
    u = jax.nn.silu(_layernorm(u, conv_ln_g, conv_ln_b))
    y_conv = u @ w_conv_out.astype(u.dtype)

    def heads(t, d):
        return jnp.transpose(t.astype(jnp.float32).reshape(B, S, MLSTM_HEADS, d), (0, 2, 1, 3))
    qh = heads(q, MLSTM_HEAD_QK) * (MLSTM_HEAD_QK ** -0.5)
    kh = heads(k, MLSTM_HEAD_QK)
    vh = heads(v, MLSTM_HEAD_V)
    gts = (gate_pre + b_gates.astype(gate_pre.dtype)).astype(jnp.float32).reshape(B, S, 4, MLSTM_HEADS)
    gts = jnp.transpose(gts, (2, 0, 3, 1))
    ig_f, lf_f, ig_b, lf_b = gts[0], jax.nn.log_sigmoid(gts[1]), gts[2], jax.nn.log_sigmoid(gts[3])
    h_f = _mlstm_chunkwise(qh, kh, vh, ig_f, lf_f)
    flip = lambda t: jnp.flip(t, axis=2)
    h_b = flip(_mlstm_chunkwise(flip(qh), flip(kh), flip(vh), flip(ig_b), flip(lf_b)))
    h = h_f + h_b
    h = h * lax.rsqrt(jnp.mean(h * h, -1, keepdims=True) + EPS)
    h = h * mlstm_head_g.astype(jnp.float32).reshape(MLSTM_HEADS, 1, MLSTM_HEAD_V)
    h = jnp.transpose(h, (0, 2, 1, 3)).reshape(B, S, MLSTM_V_DIM).astype(xn.dtype)
    h = h * jax.nn.sigmoid(o_pre)
    y_mlstm = h @ w_mlstm_out.astype(h.dtype)

    gates = jax.nn.sigmoid(merge_pre)
    mixed = gates[..., :D_MODEL] * y_conv + gates[..., D_MODEL:] * y_mlstm
    return mixed @ w_out.astype(mixed.dtype)


def _ffn(xn, w_up, ffn_dw_w, ffn_dw_b, w_down):
    up = xn @ w_up.astype(xn.dtype)
    gate, val = up[..., :D_FF], up[..., D_FF:]
    hid = jax.nn.gelu(_depthwise_conv(gate, ffn_dw_w, ffn_dw_b), approximate=False) * val
    return hid @ w_down.astype(hid.dtype)


def _trunk(x, norm_mix_g, w_in, b_gates, conv_dw_w, conv_dw_b, conv_ln_g, conv_ln_b, w_conv_out,
           mlstm_head_g, w_mlstm_out, w_out, norm_ffn_g, w_up, ffn_dw_w, ffn_dw_b, w_down, norm_final_g):
    for l in range(DEPTH):
        x = x + _mixer(_rmsnorm(x, norm_mix_g[l]), w_in[l], b_gates[l], conv_dw_w[l], conv_dw_b[l],
                       conv_ln_g[l], conv_ln_b[l], w_conv_out[l], mlstm_head_g[l], w_mlstm_out[l], w_out[l])
        x = x + _ffn(_rmsnorm(x, norm_ffn_g[l]), w_up[l], ffn_dw_w[l], ffn_dw_b[l], w_down[l])
    return _rmsnorm(x, norm_final_g)


def setup_inputs(seed: int = 0) -> dict:
    key = jax.random.key(seed)
    ks = jax.random.split(key, 20)
    nrm = lambda k, shape, s: jax.random.normal(k, shape, jnp.float32) * s
    ig_bias = nrm(ks[4], (DEPTH, 2, MLSTM_HEADS), 0.1)
    fg_bias = jnp.linspace(3.0, 6.0, MLSTM_HEADS, dtype=jnp.float32)[None, None, :] + nrm(ks[5], (DEPTH, 2, MLSTM_HEADS), 0.1)
    b_gates = jnp.stack([ig_bias[:, 0], fg_bias[:, 0], ig_bias[:, 1], fg_bias[:, 1]], axis=1).reshape(DEPTH, 4 * MLSTM_HEADS)
    return {
        'x_prompt': nrm(ks[0], (BATCH, SEQ, D_MODEL), 1.0),
        'x_sample': nrm(ks[1], (DEC_BATCH, DEC_SEQ, D_MODEL), 1.0),
        'norm_mix_g': 1.0 + nrm(ks[2], (DEPTH, D_MODEL), 0.1),
        'w_in': nrm(ks[3], (DEPTH, D_MODEL, D_IN), D_MODEL ** -0.5),
        'b_gates': b_gates,
        'conv_dw_w': nrm(ks[6], (DEPTH, CONV_WIDTH, CONV_DIM), CONV_WIDTH ** -0.5),
        'conv_dw_b': nrm(ks[7], (DEPTH, CONV_DIM), 0.02),
        'conv_ln_g': 1.0 + nrm(ks[8], (DEPTH, CONV_DIM), 0.1),
        'conv_ln_b': nrm(ks[9], (DEPTH, CONV_DIM), 0.02),
        'w_conv_out': nrm(ks[10], (DEPTH, CONV_DIM, D_MODEL), CONV_DIM ** -0.5),
        'mlstm_head_g': 1.0 + nrm(ks[11], (DEPTH, MLSTM_V_DIM), 0.1),
        'w_mlstm_out': nrm(ks[12], (DEPTH, MLSTM_V_DIM, D_MODEL), MLSTM_V_DIM ** -0.5),
        'w_out': nrm(ks[13], (DEPTH, D_MODEL, D_MODEL), D_MODEL ** -0.5),
        'norm_ffn_g': 1.0 + nrm(ks[14], (DEPTH, D_MODEL), 0.1),
        'w_up': nrm(ks[15], (DEPTH, D_MODEL, 2 * D_FF), D_MODEL ** -0.5),
        'ffn_dw_w': nrm(ks[16], (DEPTH, FFN_CONV_WIDTH, D_FF), FFN_CONV_WIDTH ** -0.5),
        'ffn_dw_b': nrm(ks[17], (DEPTH, D_FF), 0.02),
        'w_down': nrm(ks[18], (DEPTH, D_FF, D_MODEL), D_FF ** -0.5),
        'norm_final_g': 1.0 + nrm(ks[19], (D_MODEL,), 0.1),
    }


def reference(x_prompt, x_sample, norm_mix_g, w_in, b_gates, conv_dw_w, conv_dw_b, conv_ln_g, conv_ln_b,
              w_conv_out, mlstm_head_g, w_mlstm_out, w_out, norm_ffn_g, w_up, ffn_dw_w, ffn_dw_b, w_down,
              norm_final_g):
    y_prompt = _trunk(x_prompt, norm_mix_g, w_in, b_gates, conv_dw_w, conv_dw_b, conv_ln_g, conv_ln_b,
                      w_conv_out, mlstm_head_g, w_mlstm_out, w_out, norm_ffn_g, w_up, ffn_dw_w, ffn_dw_b,
                      w_down, norm_final_g)
    y_sample = _trunk(x_sample, norm_mix_g, w_in, b_gates, conv_dw_w, conv_dw_b, conv_ln_g, conv_ln_b,
                      w_conv_out, mlstm_head_g, w_mlstm_out, w_out, norm_ffn_g, w_up, ffn_dw_w, ffn_dw_b,
                      w_down, norm_final_g)
    return (y_prompt, y_sample)
```

```python
import functools

import jax
import jax.numpy as jnp
from jax import lax
from jax.experimental import pallas as pl
from jax.experimental.pallas import tpu as pltpu

F32 = jnp.float32
BF16 = jnp.bfloat16

D_MODEL = 2048
CONV_DIM = 1024
CONV_WIDTH = 31
CONV_PAD = (CONV_WIDTH - 1) // 2
HEADS = 4
DK = 256
DV = 512
QK_DIM = HEADS * DK
V_DIM = HEADS * DV
CHUNK = 128
D_FF = 5504
EPS = 1e-6

LANES = 128
BF16_SUBLANES = 16
VMEM_LIMIT = 56 * 1024 * 1024

COL_CONV = 0
COL_Q = 2 * CONV_DIM
COL_K = COL_Q + QK_DIM
COL_V = COL_K + QK_DIM
COL_O = COL_V + V_DIM
COL_MERGE = COL_O + V_DIM
D_MAIN = COL_MERGE + 2 * D_MODEL
N_GATES = 4 * HEADS
GATE_OFF = COL_O + V_DIM

IN_TM, IN_TN = 1024, 1024
CONV_TM, CONV_ROWS = 512, 32
MERGE_TM = 256
FFN_TM, FFN_TF = 512, 512
D_FF_PAD = -(-D_FF // FFN_TF) * FFN_TF
FFN_HALO = BF16_SUBLANES


def _params(*sem):
    return pltpu.CompilerParams(dimension_semantics=sem, vmem_limit_bytes=VMEM_LIMIT)


def _resident(shape):
    return pl.BlockSpec(shape, lambda *_: (0,) * len(shape), pipeline_mode=pl.Buffered(1))


def _rms(x, g):
    return x * lax.rsqrt(jnp.mean(x * x, -1, keepdims=True) + EPS) * g


def _in_proj_kernel(x_ref, g_ref, w_ref, wg_ref, proj_ref, gates_ref, xn_ref):
    @pl.when(pl.program_id(2) == 0)
    def _():
        xn = _rms(x_ref[0], g_ref[...]).astype(BF16)
        xn_ref[...] = xn
        gates_ref[0] = jnp.dot(xn, wg_ref[...], preferred_element_type=F32)

    proj_ref[0] = jnp.dot(xn_ref[...], w_ref[...], preferred_element_type=F32).astype(BF16)


def _in_proj(x, g, w_main, w_gate):
    B, S, _ = x.shape
    tm = min(IN_TM, S)
    return pl.pallas_call(
        _in_proj_kernel,
        grid=(B, S // tm, D_MAIN // IN_TN),
        in_specs=[
            pl.BlockSpec((1, tm, D_MODEL), lambda b, i, j: (b, i, 0)),
            pl.BlockSpec((1, D_MODEL), lambda b, i, j: (0, 0)),
            pl.BlockSpec((D_MODEL, IN_TN), lambda b, i, j: (0, j)),
            pl.BlockSpec((D_MODEL, LANES), lambda b, i, j: (0, 0)),
        ],
        out_specs=[
            pl.BlockSpec((1, tm, IN_TN), lambda b, i, j: (b, i, j)),
            pl.BlockSpec((1, tm, LANES), lambda b, i, j: (b, i, 0)),
        ],
        out_shape=[
            jax.ShapeDtypeStruct((B, S, D_MAIN), BF16),
            jax.ShapeDtypeStruct((B, S, LANES), F32),
        ],
        scratch_shapes=[pltpu.VMEM((tm, D_MODEL), BF16)],
        compiler_params=_params("parallel", "parallel", "arbitrary"),
        name="in_proj",
    )(x, g, w_main, w_gate)


def _log_sigmoid(x):
    return jnp.minimum(x, 0.0) - jnp.log1p(jnp.exp(-jnp.abs(x)))


def _split3(x):
    hi = x.astype(BF16)
    r = x - hi.astype(F32)
    mid = r.astype(BF16)
    lo = (r - mid.astype(F32)).astype(BF16)
    return hi, mid, lo


def _gate_prep_kernel(raw_ref, bias_ref, col_ref, row_ref):
    g = raw_ref[0] + bias_ref[...]
    lane = lax.broadcasted_iota(jnp.int32, g.shape, 1)
    is_lf_f = (lane >= HEADS) & (lane < 2 * HEADS)
    is_lf_b = (lane >= 3 * HEADS) & (lane < 4 * HEADS)
    g = jnp.where(is_lf_f | is_lf_b, _log_sigmoid(g), g)
    r = lax.broadcasted_iota(jnp.int32, (CHUNK, CHUNK), 0)
    c = lax.broadcasted_iota(jnp.int32, (CHUNK, CHUNK), 1)
    tril = (c <= r).astype(BF16)
    triu = (c >= r).astype(BF16)
    cs_f = jnp.zeros_like(g)
    cs_b = jnp.zeros_like(g)
    for part in _split3(g):
        cs_f = cs_f + jnp.dot(tril, part, preferred_element_type=F32)
        cs_b = cs_b + jnp.dot(triu, part, preferred_element_type=F32)
    col = jnp.where(is_lf_f, cs_f, jnp.where(is_lf_b, cs_b, g))
    col_ref[0] = col
    row_ref[0] = col.T[:N_GATES, :]


def _gate_prep(raw, bias):
    B, S, _ = raw.shape
    return pl.pallas_call(
        _gate_prep_kernel,
        grid=(B, S // CHUNK),
        in_specs=[
            pl.BlockSpec((1, CHUNK, LANES), lambda b, c: (b, c, 0)),
            pl.BlockSpec((1, LANES), lambda b, c: (0, 0)),
        ],
        out_specs=[
            pl.BlockSpec((1, CHUNK, LANES), lambda b, c: (b, c, 0)),
            pl.BlockSpec((1, N_GATES, CHUNK), lambda b, c: (b, 0, c)),
        ],
        out_shape=[
            jax.ShapeDtypeStruct((B, S, LANES), F32),
            jax.ShapeDtypeStruct((B, N_GATES, S), F32),
        ],
        compiler_params=_params("parallel", "parallel"),
        name="gate_prep",
    )(raw, bias)


def _conv_kernel(main_ref, prev_ref, next_ref, w_ref, b_ref, lg_ref, lb_ref, out_ref, u_ref, *, tm):
    i = pl.program_id(1)
    halo = BF16_SUBLANES

    def glu(blk):
        return blk[:, :CONV_DIM].astype(F32) * jax.nn.sigmoid(blk[:, CONV_DIM:].astype(F32))

    u_ref[0:halo, :] = jnp.where(i > 0, glu(prev_ref[0]), 0.0)
    u_ref[halo:halo + tm, :] = glu(main_ref[0])
    u_ref[halo + tm:2 * halo + tm, :] = jnp.where(i < pl.num_programs(1) - 1, glu(next_ref[0]), 0.0)

    for r0 in range(0, tm, CONV_ROWS):
        acc = jnp.broadcast_to(b_ref[...], (CONV_ROWS, CONV_DIM))
        for j in range(CONV_WIDTH):
            lo = r0 + halo - CONV_PAD + j
            acc = acc + w_ref[j:j + 1, :] * u_ref[lo:lo + CONV_ROWS, :]
        mu = jnp.mean(acc, -1, keepdims=True)
        xc = acc - mu
        y = xc * lax.rsqrt(jnp.mean(xc * xc, -1, keepdims=True) + EPS) * lg_ref[...] + lb_ref[...]
        out_ref[0, r0:r0 + CONV_ROWS, :] = (y * jax.nn.sigmoid(y)).astype(BF16)


def _conv_branch(proj, w, b, lg, lb):
    B, S, _ = proj.shape
    tm = min(CONV_TM, S)
    halo = BF16_SUBLANES
    per = tm // halo
    last = S // halo - 1
    return pl.pallas_call(
        functools.partial(_conv_kernel, tm=tm),
        grid=(B, S // tm),
        in_specs=[
            pl.BlockSpec((1, tm, 2 * CONV_DIM), lambda b, i: (b, i, 0)),
            pl.BlockSpec((1, halo, 2 * CONV_DIM), lambda b, i: (b, jnp.maximum(i * per - 1, 0), 0)),
            pl.BlockSpec((1, halo, 2 * CONV_DIM), lambda b, i: (b, jnp.minimum((i + 1) * per, last), 0)),
            pl.BlockSpec((CONV_WIDTH + 1, CONV_DIM), lambda b, i: (0, 0)),
            pl.BlockSpec((1, CONV_DIM), lambda b, i: (0, 0)),
            pl.BlockSpec((1, CONV_DIM), lambda b, i: (0, 0)),
            pl.BlockSpec((1, CONV_DIM), lambda b, i: (0, 0)),
        ],
        out_specs=pl.BlockSpec((1, tm, CONV_DIM), lambda b, i: (b, i, 0)),
        out_shape=jax.ShapeDtypeStruct((B, S, CONV_DIM), BF16),
        scratch_shapes=[pltpu.VMEM((tm + 2 * halo, CONV_DIM), F32)],
        compiler_params=_params("parallel", "parallel"),
        name="conv_branch",
    )(proj, proj, proj, w, b, lg, lb)


def _mlstm_cell(q, k, v, ig_col, b_col, ig_row, b_row, b_last, mask, c_ref, n_ref, m_ref, idx):
    scale = DK ** -0.5
    m = m_ref[idx][:, 0:1]
    c_state = c_ref[idx]
    n_state = n_ref[idx]

    d = jnp.where(mask, b_col - b_row + ig_row, -jnp.inf)
    a = b_col + m
    m_t = jnp.maximum(a, jnp.max(d, -1, keepdims=True))
    w = jnp.exp(d - m_t)
    qk = lax.dot_general(q, k, (((1,), (1,)), ((), ())), preferred_element_type=F32)
    s = qk * scale * w
    inter = jnp.exp(a - m_t)
    q_c = jnp.dot(q, c_state.astype(BF16), preferred_element_type=F32) * scale
    num = inter * q_c + jnp.dot(s.astype(BF16), v, preferred_element_type=F32)
    q_n = jnp.sum(q.astype(F32) * n_state, -1, keepdims=True) * scale
    den = inter * q_n + jnp.sum(s, -1, keepdims=True)
    h = num * (1.0 / jnp.maximum(jnp.abs(den), jnp.exp(-m_t)))

    g_row = b_last - b_row + ig_row
    g_col = b_last - b_col + ig_col
    m_new = jnp.maximum(b_last + m, jnp.max(g_row, -1, keepdims=True))
    decay = jnp.exp(b_last + m - m_new)
    wk = jnp.exp(g_col - m_new) * k.astype(F32)
    c_ref[idx] = decay * c_state + lax.dot_general(
        wk.astype(BF16), v, (((0,), (0,)), ((), ())), preferred_element_type=F32)
    n_ref[idx] = decay * n_state + jnp.sum(wk, 0, keepdims=True)
    m_ref[idx] = jnp.broadcast_to(m_new, (1, LANES))
    return h


def _mlstm_kernel(qf_ref, kf_ref, vf_ref, qb_ref, kb_ref, vb_ref, colf_ref, colb_ref, rowf_ref, rowb_ref,
                  hf_ref, hb_ref, c_ref, n_ref, m_ref):
    @pl.when(pl.program_id(1) == 0)
    def _():
        c_ref[...] = jnp.zeros_like(c_ref)
        n_ref[...] = jnp.zeros_like(n_ref)
        m_ref[...] = jnp.zeros_like(m_ref)

    r = lax.broadcasted_iota(jnp.int32, (CHUNK, CHUNK), 0)
    c = lax.broadcasted_iota(jnp.int32, (CHUNK, CHUNK), 1)
    dirs = (
        (qf_ref, kf_ref, vf_ref, colf_ref, rowf_ref, hf_ref, c <= r, CHUNK - 1, 0),
        (qb_ref, kb_ref, vb_ref, colb_ref, rowb_ref, hb_ref, c >= r, 0, 2 * HEADS),
    )
    for d, (q_ref, k_ref, v_ref, col_ref, row_ref, h_ref, mask, last, goff) in enumerate(dirs):
        for h in range(HEADS):
            ig, bb = goff + h, goff + HEADS + h
            b_row = row_ref[0, bb:bb + 1, :]
            out = _mlstm_cell(
                q_ref[0, :, h * DK:(h + 1) * DK], k_ref[0, :, h * DK:(h + 1) * DK],
                v_ref[0, :, h * DV:(h + 1) * DV],
                col_ref[0, :, ig:ig + 1], col_ref[0, :, bb:bb + 1],
                row_ref[0, ig:ig + 1, :], b_row, b_row[:, last:last + 1],
                mask, c_ref, n_ref, m_ref, d * HEADS + h)
            h_ref[0, :, h * DV:(h + 1) * DV] = out.astype(BF16)


def _mlstm(proj, col_g, row_g):
    B, S, _ = proj.shape
    nc = S // CHUNK
    fwd = lambda blk: (lambda b, c: (b, c, blk))
    bwd = lambda blk: (lambda b, c: (b, nc - 1 - c, blk))
    qkv = lambda im: [
        pl.BlockSpec((1, CHUNK, QK_DIM), im(COL_Q // QK_DIM)),
        pl.BlockSpec((1, CHUNK, QK_DIM), im(COL_K // QK_DIM)),
        pl.BlockSpec((1, CHUNK, V_DIM), im(COL_V // V_DIM)),
    ]
    return pl.pallas_call(
        _mlstm_kernel,
        grid=(B, nc),
        in_specs=qkv(fwd) + qkv(bwd) + [
            pl.BlockSpec((1, CHUNK, LANES), fwd(0)),
            pl.BlockSpec((1, CHUNK, LANES), bwd(0)),
            pl.BlockSpec((1, N_GATES, CHUNK), lambda b, c: (b, 0, c)),
            pl.BlockSpec((1, N_GATES, CHUNK), lambda b, c: (b, 0, nc - 1 - c)),
        ],
        out_specs=[
            pl.BlockSpec((1, CHUNK, V_DIM), fwd(0)),
            pl.BlockSpec((1, CHUNK, V_DIM), bwd(0)),
        ],
        out_shape=[jax.ShapeDtypeStruct((B, S, V_DIM), BF16)] * 2,
        scratch_shapes=[
            pltpu.VMEM((2 * HEADS, DK, DV), F32),
            pltpu.VMEM((2 * HEADS, 1, DK), F32),
            pltpu.VMEM((2 * HEADS, 1, LANES), F32),
        ],
        compiler_params=_params("parallel", "arbitrary"),
        name="mlstm",
    )(proj, proj, proj, proj, proj, proj, col_g, col_g, row_g, row_g)


def _merge_kernel(hf_ref, hb_ref, o_ref, mg_ref, u_ref, x_ref, hg_ref, wc_ref, wm_ref, wo_ref, out_ref):
    hs = hf_ref[0].astype(F32) + hb_ref[0].astype(F32)
    heads = []
    for h in range(HEADS):
        blk = hs[:, h * DV:(h + 1) * DV]
        heads.append(_rms(blk, hg_ref[:, h * DV:(h + 1) * DV]))
    hn = jnp.concatenate(heads, -1) * jax.nn.sigmoid(o_ref[0].astype(F32))
    y_mlstm = jnp.dot(hn.astype(BF16), wm_ref[...], preferred_element_type=F32)
    y_conv = jnp.dot(u_ref[0], wc_ref[...], preferred_element_type=F32)
    mg = mg_ref[0]
    mixed = (jax.nn.sigmoid(mg[:, :D_MODEL].astype(F32)) * y_conv
             + jax.nn.sigmoid(mg[:, D_MODEL:].astype(F32)) * y_mlstm)
    out_ref[0] = x_ref[0] + jnp.dot(mixed.astype(BF16), wo_ref[...], preferred_element_type=F32)


def _merge(hf, hb, proj, u, x, head_g, w_conv_out, w_mlstm_out, w_out):
    B, S, _ = x.shape
    tm = min(MERGE_TM, S)
    row = lambda width, blk: pl.BlockSpec((1, tm, width), lambda b, i: (b, i, blk))
    return pl.pallas_call(
        _merge_kernel,
        grid=(B, S // tm),
        in_specs=[
            row(V_DIM, 0), row(V_DIM, 0),
            row(V_DIM, COL_O // V_DIM),
            row(2 * D_MODEL, COL_MERGE // (2 * D_MODEL)),
            row(CONV_DIM, 0),
            row(D_MODEL, 0),
            _resident((1, V_DIM)),
            _resident((CONV_DIM, D_MODEL)),
            _resident((V_DIM, D_MODEL)),
            _resident((D_MODEL, D_MODEL)),
        ],
        out_specs=row(D_MODEL, 0),
        out_shape=jax.ShapeDtypeStruct((B, S, D_MODEL), F32),
        compiler_params=_params("parallel", "parallel"),
        name="merge",
    )(hf, hb, proj, proj, u, x, head_g, w_conv_out, w_mlstm_out, w_out)


def _ffn_kernel(x_ref, prev_ref, next_ref, g_ref, wg_ref, wv_ref, cw_ref, cb_ref, wd_ref, gf_ref,
                out_ref, xn_ref, *, tm, final_norm):
    i, j = pl.program_id(1), pl.program_id(2)
    halo = FFN_HALO
    rows = tm + 2 * halo

    @pl.when(j == 0)
    def _():
        g = g_ref[...]
        xn_ref[0:halo, :] = jnp.where(i > 0, _rms(prev_ref[0], g), 0.0).astype(BF16)
        xn_ref[halo:halo + tm, :] = _rms(x_ref[0], g).astype(BF16)
        xn_ref[halo + tm:rows, :] = jnp.where(i < pl.num_programs(1) - 1, _rms(next_ref[0], g), 0.0).astype(BF16)
        out_ref[0] = x_ref[0]

    gate = jnp.dot(xn_ref[...], wg_ref[...], preferred_element_type=F32)
    val = jnp.dot(xn_ref[halo:halo + tm, :], wv_ref[...], preferred_element_type=F32)
    conv = (cw_ref[0:1, :] * pltpu.roll(gate, 1, 0)[halo:halo + tm]
            + cw_ref[1:2, :] * gate[halo:halo + tm]
            + cw_ref[2:3, :] * pltpu.roll(gate, rows - 1, 0)[halo:halo + tm]
            + cb_ref[...])
    hid = 0.5 * conv * (1.0 + lax.erf(conv * (2.0 ** -0.5))) * val
    out_ref[0] += jnp.dot(hid.astype(BF16), wd_ref[...], preferred_element_type=F32)

    if final_norm:
        @pl.when(j == pl.num_programs(2) - 1)
        def _():
            out_ref[0] = _rms(out_ref[0], gf_ref[...])


def _ffn(x, g, w_gate, w_val, conv_w, conv_b, w_down, g_final, final_norm):
    B, S, _ = x.shape
    tm = min(FFN_TM, S)
    halo = FFN_HALO
    per = tm // halo
    last = S // halo - 1
    return pl.pallas_call(
        functools.partial(_ffn_kernel, tm=tm, final_norm=final_norm),
        grid=(B, S // tm, D_FF_PAD // FFN_TF),
        in_specs=[
            pl.BlockSpec((1, tm, D_MODEL), lambda b, i, j: (b, i, 0)),
            pl.BlockSpec((1, halo, D_MODEL), lambda b, i, j: (b, jnp.maximum(i * per - 1, 0), 0)),
            pl.BlockSpec((1, halo, D_MODEL), lambda b, i, j: (b, jnp.minimum((i + 1) * per, last), 0)),
            pl.BlockSpec((1, D_MODEL), lambda b, i, j: (0, 0)),
            pl.BlockSpec((D_MODEL, FFN_TF), lambda b, i, j: (0, j)),
            pl.BlockSpec((D_MODEL, FFN_TF), lambda b, i, j: (0, j)),
            pl.BlockSpec((8, FFN_TF), lambda b, i, j: (0, j)),
            pl.BlockSpec((1, FFN_TF), lambda b, i, j: (0, j)),
            pl.BlockSpec((FFN_TF, D_MODEL), lambda b, i, j: (j, 0)),
            pl.BlockSpec((1, D_MODEL), lambda b, i, j: (0, 0)),
        ],
        out_specs=pl.BlockSpec((1, tm, D_MODEL), lambda b, i, j: (b, i, 0)),
        out_shape=jax.ShapeDtypeStruct((B, S, D_MODEL), F32),
        scratch_shapes=[pltpu.VMEM((tm + 2 * halo, D_MODEL), BF16)],
        compiler_params=_params("parallel", "parallel", "arbitrary"),
        name="ffn",
    )(x, x, x, g, w_gate, w_val, conv_w, conv_b, w_down, g_final)


def _pad_to(a, axis, size):
    pad = [(0, 0)] * a.ndim
    pad[axis] = (0, size - a.shape[axis])
    return jnp.pad(a, pad)


def _layer_params(l, norm_mix_g, w_in, b_gates, conv_dw_w, conv_dw_b, conv_ln_g, conv_ln_b, w_conv_out,
                  mlstm_head_g, w_mlstm_out, w_out, norm_ffn_g, w_up, ffn_dw_w, ffn_dw_b, w_down):
    wi = w_in[l]
    row = lambda a: a[l].reshape(1, -1)
    return dict(
        norm_mix_g=row(norm_mix_g),
        w_main=jnp.concatenate([wi[:, :GATE_OFF], wi[:, GATE_OFF + N_GATES:]], 1).astype(BF16),
        w_gate=_pad_to(wi[:, GATE_OFF:GATE_OFF + N_GATES], 1, LANES).astype(BF16),
        b_gates=_pad_to(row(b_gates), 1, LANES),
        conv_w=_pad_to(conv_dw_w[l], 0, CONV_WIDTH + 1),
        conv_b=row(conv_dw_b), ln_g=row(conv_ln_g), ln_b=row(conv_ln_b),
        w_conv_out=w_conv_out[l].astype(BF16),
        head_g=row(mlstm_head_g),
        w_mlstm_out=w_mlstm_out[l].astype(BF16),
        w_out=w_out[l].astype(BF16),
        norm_ffn_g=row(norm_ffn_g),
        w_up_gate=_pad_to(w_up[l][:, :D_FF], 1, D_FF_PAD).astype(BF16),
        w_up_val=_pad_to(w_up[l][:, D_FF:], 1, D_FF_PAD).astype(BF16),
        ffn_w=_pad_to(_pad_to(ffn_dw_w[l], 0, 8), 1, D_FF_PAD),
        ffn_b=_pad_to(row(ffn_dw_b), 1, D_FF_PAD),
        w_down=_pad_to(w_down[l], 0, D_FF_PAD).astype(BF16),
    )


def _trunk(x, layers, g_final):
    for l, p in enumerate(layers):
        proj, gates = _in_proj(x, p["norm_mix_g"], p["w_main"], p["w_gate"])
        col_g, row_g = _gate_prep(gates, p["b_gates"])
        u = _conv_branch(proj, p["conv_w"], p["conv_b"], p["ln_g"], p["ln_b"])
        hf, hb = _mlstm(proj, col_g, row_g)
        x = _merge(hf, hb, proj, u, x, p["head_g"], p["w_conv_out"], p["w_mlstm_out"], p["w_out"])
        x = _ffn(x, p["norm_ffn_g"], p["w_up_gate"], p["w_up_val"], p["ffn_w"], p["ffn_b"], p["w_down"],
                 g_final, final_norm=(l == len(layers) - 1))
    return x


def kernel(x_prompt, x_sample, norm_mix_g, w_in, b_gates, conv_dw_w, conv_dw_b, conv_ln_g, conv_ln_b,
           w_conv_out, mlstm_head_g, w_mlstm_out, w_out, norm_ffn_g, w_up, ffn_dw_w, ffn_dw_b, w_down,
           norm_final_g):
    depth = w_in.shape[0]
    layers = [
        _layer_params(l, norm_mix_g, w_in, b_gates, conv_dw_w, conv_dw_b, conv_ln_g, conv_ln_b, w_conv_out,
                      mlstm_head_g, w_mlstm_out, w_out, norm_ffn_g, w_up, ffn_dw_w, ffn_dw_b, w_down)
        for l in range(depth)
    ]
    g_final = norm_final_g.reshape(1, -1)
    return (_trunk(x_prompt, layers, g_final), _trunk(x_sample, layers, g_final))
```

```python
import functools

import jax
import jax.numpy as jnp
from jax import lax
from jax.experimental import pallas as pl
from jax.experimental.pallas import tpu as pltpu

F32 = jnp.float32
BF16 = jnp.bfloat16

D_MODEL = 2048
CONV_DIM = 1024
CONV_WIDTH = 31
CONV_PAD = (CONV_WIDTH - 1) // 2
HEADS = 4
CELLS = 2 * HEADS
DK = 256
DV = 512
QK_DIM = HEADS * DK
V_DIM = HEADS * DV
CHUNK = 128
D_FF = 5504
FFN_CONV_WIDTH = 3
EPS = 1e-6

LANES = 128
SUBLANES = 8
BF16_SUBLANES = 16
VMEM_LIMIT = 56 * 1024 * 1024

COL_CONV = 0
COL_Q = 2 * CONV_DIM
COL_K = COL_Q + QK_DIM
COL_V = COL_K + QK_DIM
COL_O = COL_V + V_DIM
COL_MERGE = COL_O + V_DIM
D_MAIN = COL_MERGE + 2 * D_MODEL
N_GATES = 4 * HEADS
GATE_OFF = COL_O + V_DIM
GATE_W = 2 * LANES

G_ALPHA, G_INTER, G_FLOOR, G_KSCALE, G_GROUPS = 0, 1, 2, 3, 4

IN_TM, IN_TN = 1024, 1024
SCAN_CHUNKS = 4
CONV_TM, CONV_ROWS = 512, 32
CONV_HALO = BF16_SUBLANES
MERGE_TM = 256
FFN_TM, FFN_TF = 512, 512
D_FF_PAD = -(-D_FF // FFN_TF) * FFN_TF
FFN_HALO = SUBLANES


def _params(*sem):
    return pltpu.CompilerParams(dimension_semantics=sem, vmem_limit_bytes=VMEM_LIMIT)


def _resident(shape):
    return pl.BlockSpec(shape, lambda *_: (0,) * len(shape), pipeline_mode=pl.Buffered(1))


def _rms(x, g):
    return x * lax.rsqrt(jnp.mean(x * x, -1, keepdims=True) + EPS) * g


def _in_proj_kernel(x_ref, g_ref, w_ref, wg_ref, proj_ref, gates_ref, xn_ref):
    @pl.when(pl.program_id(2) == 0)
    def _():
        xn = _rms(x_ref[0], g_ref[...]).astype(BF16)
        xn_ref[...] = xn
        gates_ref[0] = jnp.dot(xn, wg_ref[...], preferred_element_type=F32)

    proj_ref[0] = jnp.dot(xn_ref[...], w_ref[...], preferred_element_type=F32).astype(BF16)


def _in_proj(x, g, w_main, w_gate):
    B, S, _ = x.shape
    tm = min(IN_TM, S)
    return pl.pallas_call(
        _in_proj_kernel,
        grid=(B, S // tm, D_MAIN // IN_TN),
        in_specs=[
            pl.BlockSpec((1, tm, D_MODEL), lambda b, i, j: (b, i, 0)),
            pl.BlockSpec((1, D_MODEL), lambda b, i, j: (0, 0)),
            pl.BlockSpec((D_MODEL, IN_TN), lambda b, i, j: (0, j)),
            pl.BlockSpec((D_MODEL, GATE_W), lambda b, i, j: (0, 0)),
        ],
        out_specs=[
            pl.BlockSpec((1, tm, IN_TN), lambda b, i, j: (b, i, j)),
            pl.BlockSpec((1, tm, GATE_W), lambda b, i, j: (b, i, 0)),
        ],
        out_shape=[
            jax.ShapeDtypeStruct((B, S, D_MAIN), BF16),
            jax.ShapeDtypeStruct((B, S, GATE_W), F32),
        ],
        scratch_shapes=[pltpu.VMEM((tm, D_MODEL), BF16)],
        compiler_params=_params("parallel", "parallel", "arbitrary"),
        name="in_proj",
    )(x, g, w_main, w_gate)


def _log_sigmoid(x):
    return jnp.minimum(x, 0.0) - jnp.log1p(jnp.exp(-jnp.abs(x)))


def _split3(x):
    hi = x.astype(BF16)
    r = x - hi.astype(F32)
    mid = r.astype(BF16)
    lo = (r - mid.astype(F32)).astype(BF16)
    return hi, mid, lo


def _gate_scan_kernel(rawf_ref, rawb_ref, bias_ref, tab_ref, row_ref, dec_ref, m_ref):
    @pl.when(pl.program_id(1) == 0)
    def _():
        m_ref[...] = jnp.zeros_like(m_ref)

    n = CHUNK
    lane = lax.broadcasted_iota(jnp.int32, (n, LANES), 1)
    row_id = lax.broadcasted_iota(jnp.int32, (n, LANES), 0)
    is_f = lane < HEADS
    r = lax.broadcasted_iota(jnp.int32, (n, n), 0)
    c = lax.broadcasted_iota(jnp.int32, (n, n), 1)
    tril = (c <= r).astype(BF16)
    triu = (c >= r).astype(BF16)
    m = m_ref[...]
    for i in range(SCAN_CHUNKS):
        lo, hi = i * n, (i + 1) * n
        rf = rawf_ref[0, lo:hi, :]
        rb = rawb_ref[0, (SCAN_CHUNKS - 1 - i) * n:(SCAN_CHUNKS - i) * n, :]
        ig = jnp.where(is_f, rf[:, :LANES], rb[:, :LANES]) + bias_ref[:, :LANES]
        lf = _log_sigmoid(jnp.where(is_f, rf[:, LANES:], rb[:, LANES:]) + bias_ref[:, LANES:])
        cs_f = jnp.zeros_like(lf)
        cs_b = jnp.zeros_like(lf)
        for part in _split3(lf):
            cs_f = cs_f + jnp.dot(tril, part, preferred_element_type=F32)
            cs_b = cs_b + jnp.dot(triu, part, preferred_element_type=F32)
        b = jnp.where(is_f, cs_f, cs_b)
        cc = ig - b
        cm = cc
        k = 1
        while k < n:
            dn = jnp.where(row_id >= k, pltpu.roll(cm, k, 0), -jnp.inf)
            up = jnp.where(row_id < n - k, pltpu.roll(cm, n - k, 0), -jnp.inf)
            cm = jnp.maximum(cm, jnp.where(is_f, dn, up))
            k *= 2
        mm = jnp.maximum(m, cm)
        total = jnp.where(is_f[:1], b[n - 1:n], b[0:1])
        m_end = jnp.where(is_f[:1], mm[n - 1:n], mm[0:1])
        tab_ref[0, lo:hi, G_ALPHA * LANES:(G_ALPHA + 1) * LANES] = -mm
        tab_ref[0, lo:hi, G_INTER * LANES:(G_INTER + 1) * LANES] = jnp.exp(m - mm)
        tab_ref[0, lo:hi, G_FLOOR * LANES:(G_FLOOR + 1) * LANES] = jnp.exp(-(b + mm))
        tab_ref[0, lo:hi, G_KSCALE * LANES:(G_KSCALE + 1) * LANES] = jnp.exp(cc - m_end)
        row_ref[0, :, lo:hi] = cc.T[:CELLS, :]
        dec_ref[0, i] = jnp.exp(m - m_end)
        m = total + m_end
    m_ref[...] = m


def _gate_scan(raw, bias):
    B, S, _ = raw.shape
    rows = SCAN_CHUNKS * CHUNK
    nj = S // rows
    return pl.pallas_call(
        _gate_scan_kernel,
        grid=(B, nj),
        in_specs=[
            pl.BlockSpec((1, rows, GATE_W), lambda b, j: (b, j, 0)),
            pl.BlockSpec((1, rows, GATE_W), lambda b, j: (b, nj - 1 - j, 0)),
            pl.BlockSpec((1, GATE_W), lambda b, j: (0, 0)),
        ],
        out_specs=[
            pl.BlockSpec((1, rows, G_GROUPS * LANES), lambda b, j: (b, j, 0)),
            pl.BlockSpec((1, CELLS, rows), lambda b, j: (b, 0, j)),
            pl.BlockSpec((1, SCAN_CHUNKS, 1, LANES), lambda b, j: (b, j, 0, 0)),
        ],
        out_shape=[
            jax.ShapeDtypeStruct((B, S, G_GROUPS * LANES), F32),
            jax.ShapeDtypeStruct((B, CELLS, S), F32),
            jax.ShapeDtypeStruct((B, S // CHUNK, 1, LANES), F32),
        ],
        scratch_shapes=[pltpu.VMEM((1, LANES), F32)],
        compiler_params=_params("parallel", "arbitrary"),
        name="gate_scan",
    )(raw, raw, bias)


def _conv_kernel(main_ref, prev_ref, next_ref, w_ref, b_ref, lg_ref, lb_ref, out_ref, u_ref, *, tm):
    i = pl.program_id(1)
    halo = CONV_HALO
    span = tm + 2 * halo - SUBLANES

    def glu(blk):
        return blk[:, :CONV_DIM].astype(F32) * jax.nn.sigmoid(blk[:, CONV_DIM:].astype(F32))

    u_ref[0, 0:halo, :] = jnp.where(i > 0, glu(prev_ref[0]), 0.0)
    u_ref[0, halo:halo + tm, :] = glu(main_ref[0])
    u_ref[0, halo + tm:2 * halo + tm, :] = jnp.where(i < pl.num_programs(1) - 1, glu(next_ref[0]), 0.0)
    full = u_ref[0]
    for s in range(1, SUBLANES):
        u_ref[s, 0:span, :] = pltpu.roll(full, tm + 2 * halo - s, 0)[0:span]

    for r0 in range(0, tm, CONV_ROWS):
        acc = jnp.broadcast_to(b_ref[...], (CONV_ROWS, CONV_DIM))
        for j in range(CONV_WIDTH):
            off = r0 + halo - CONV_PAD + j
            lo = off - off % SUBLANES
            acc = acc + w_ref[j:j + 1, :] * u_ref[off % SUBLANES, lo:lo + CONV_ROWS, :]
        mu = jnp.mean(acc, -1, keepdims=True)
        xc = acc - mu
        y = xc * lax.rsqrt(jnp.mean(xc * xc, -1, keepdims=True) + EPS) * lg_ref[...] + lb_ref[...]
        out_ref[0, r0:r0 + CONV_ROWS, :] = (y * jax.nn.sigmoid(y)).astype(BF16)


def _conv_branch(proj, w, b, lg, lb):
    B, S, _ = proj.shape
    tm = min(CONV_TM, S)
    halo = CONV_HALO
    per = tm // halo
    last = S // halo - 1
    return pl.pallas_call(
        functools.partial(_conv_kernel, tm=tm),
        grid=(B, S // tm),
        in_specs=[
            pl.BlockSpec((1, tm, 2 * CONV_DIM), lambda b, i: (b, i, 0)),
            pl.BlockSpec((1, halo, 2 * CONV_DIM), lambda b, i: (b, jnp.maximum(i * per - 1, 0), 0)),
            pl.BlockSpec((1, halo, 2 * CONV_DIM), lambda b, i: (b, jnp.minimum((i + 1) * per, last), 0)),
            pl.BlockSpec((CONV_WIDTH + 1, CONV_DIM), lambda b, i: (0, 0)),
            pl.BlockSpec((1, CONV_DIM), lambda b, i: (0, 0)),
            pl.BlockSpec((1, CONV_DIM), lambda b, i: (0, 0)),
            pl.BlockSpec((1, CONV_DIM), lambda b, i: (0, 0)),
        ],
        out_specs=pl.BlockSpec((1, tm, CONV_DIM), lambda b, i: (b, i, 0)),
        out_shape=jax.ShapeDtypeStruct((B, S, CONV_DIM), BF16),
        scratch_shapes=[pltpu.VMEM((SUBLANES, tm + 2 * halo, CONV_DIM), F32)],
        compiler_params=_params("parallel", "parallel"),
        name="conv_branch",
    )(proj, proj, proj, w, b, lg, lb)


def _mlstm_cell(q, k, v, alpha, c_row, inter, floor, kscale, decay, mask, c_ref, n_ref, idx):
    scale = DK ** -0.5
    c_state = c_ref[idx]
    n_state = n_ref[idx]
    w = jnp.exp(jnp.where(mask, alpha + c_row, -jnp.inf))
    qk = lax.dot_general(q, k, (((1,), (1,)), ((), ())), preferred_element_type=F32)
    s = qk * scale * w
    q_c = jnp.dot(q, c_state.astype(BF16), preferred_element_type=F32) * scale
    num = inter * q_c + jnp.dot(s.astype(BF16), v, preferred_element_type=F32)
    q_n = jnp.sum(q.astype(F32) * n_state, -1, keepdims=True) * scale
    den = inter * q_n + jnp.sum(s, -1, keepdims=True)
    h = num * (1.0 / jnp.maximum(jnp.abs(den), floor))
    wk = kscale * k.astype(F32)
    c_ref[idx] = decay * c_state + lax.dot_general(
        wk.astype(BF16), v, (((0,), (0,)), ((), ())), preferred_element_type=F32)
    n_ref[idx] = decay * n_state + jnp.sum(wk, 0, keepdims=True)
    return h


def _mlstm_kernel(qf_ref, kf_ref, vf_ref, qb_ref, kb_ref, vb_ref, tab_ref, row_ref, dec_ref,
                  hf_ref, hb_ref, c_ref, n_ref):
    @pl.when(pl.program_id(1) == 0)
    def _():
        c_ref[...] = jnp.zeros_like(c_ref)
        n_ref[...] = jnp.zeros_like(n_ref)

    r = lax.broadcasted_iota(jnp.int32, (CHUNK, CHUNK), 0)
    c = lax.broadcasted_iota(jnp.int32, (CHUNK, CHUNK), 1)
    dirs = ((qf_ref, kf_ref, vf_ref, hf_ref, c <= r), (qb_ref, kb_ref, vb_ref, hb_ref, c >= r))
    decay = dec_ref[0, 0]
    for d, (q_ref, k_ref, v_ref, h_ref, mask) in enumerate(dirs):
        for h in range(HEADS):
            cell = d * HEADS + h
            col = lambda grp: tab_ref[0, :, grp * LANES + cell:grp * LANES + cell + 1]
            out = _mlstm_cell(
                q_ref[0, :, h * DK:(h + 1) * DK], k_ref[0, :, h * DK:(h + 1) * DK],
                v_ref[0, :, h * DV:(h + 1) * DV],
                col(G_ALPHA), row_ref[0, cell:cell + 1, :], col(G_INTER), col(G_FLOOR), col(G_KSCALE),
                decay[:, cell:cell + 1], mask, c_ref, n_ref, cell)
            h_ref[0, :, h * DV:(h + 1) * DV] = out.astype(BF16)


def _mlstm(proj, tab, row_c, dec):
    B, S, _ = proj.shape
    nc = S // CHUNK
    fwd = lambda blk: (lambda b, c: (b, c, blk))
    bwd = lambda blk: (lambda b, c: (b, nc - 1 - c, blk))
    qkv = lambda im: [
        pl.BlockSpec((1, CHUNK, QK_DIM), im(COL_Q // QK_DIM)),
        pl.BlockSpec((1, CHUNK, QK_DIM), im(COL_K // QK_DIM)),
        pl.BlockSpec((1, CHUNK, V_DIM), im(COL_V // V_DIM)),
    ]
    return pl.pallas_call(
        _mlstm_kernel,
        grid=(B, nc),
        in_specs=qkv(fwd) + qkv(bwd) + [
            pl.BlockSpec((1, CHUNK, G_GROUPS * LANES), fwd(0)),
            pl.BlockSpec((1, CELLS, CHUNK), lambda b, c: (b, 0, c)),
            pl.BlockSpec((1, 1, 1, LANES), lambda b, c: (b, c, 0, 0)),
        ],
        out_specs=[
            pl.BlockSpec((1, CHUNK, V_DIM), fwd(0)),
            pl.BlockSpec((1, CHUNK, V_DIM), bwd(0)),
        ],
        out_shape=[jax.ShapeDtypeStruct((B, S, V_DIM), BF16)] * 2,
        scratch_shapes=[
            pltpu.VMEM((CELLS, DK, DV), F32),
            pltpu.VMEM((CELLS, 1, DK), F32),
        ],
        compiler_params=_params("parallel", "arbitrary"),
        name="mlstm",
    )(proj, proj, proj, proj, proj, proj, tab, row_c, dec)


def _merge_kernel(hf_ref, hb_ref, o_ref, mg_ref, u_ref, x_ref, hg_ref, wc_ref, wm_ref, wo_ref, out_ref):
    hs = hf_ref[0].astype(F32) + hb_ref[0].astype(F32)
    heads = []
    for h in range(HEADS):
        blk = hs[:, h * DV:(h + 1) * DV]
        heads.append(_rms(blk, hg_ref[:, h * DV:(h + 1) * DV]))
    hn = jnp.concatenate(heads, -1) * jax.nn.sigmoid(o_ref[0].astype(F32))
    y_mlstm = jnp.dot(hn.astype(BF16), wm_ref[...], preferred_element_type=F32)
    y_conv = jnp.dot(u_ref[0], wc_ref[...], preferred_element_type=F32)
    mg = mg_ref[0]
    mixed = (jax.nn.sigmoid(mg[:, :D_MODEL].astype(F32)) * y_conv
             + jax.nn.sigmoid(mg[:, D_MODEL:].astype(F32)) * y_mlstm)
    out_ref[0] = x_ref[0] + jnp.dot(mixed.astype(BF16), wo_ref[...], preferred_element_type=F32)


def _merge(hf, hb, proj, u, x, head_g, w_conv_out, w_mlstm_out, w_out):
    B, S, _ = x.shape
    tm = min(MERGE_TM, S)
    row = lambda width, blk: pl.BlockSpec((1, tm, width), lambda b, i: (b, i, blk))
    return pl.pallas_call(
        _merge_kernel,
        grid=(B, S // tm),
        in_specs=[
            row(V_DIM, 0), row(V_DIM, 0),
            row(V_DIM, COL_O // V_DIM),
            row(2 * D_MODEL, COL_MERGE // (2 * D_MODEL)),
            row(CONV_DIM, 0),
            row(D_MODEL, 0),
            _resident((1, V_DIM)),
            _resident((CONV_DIM, D_MODEL)),
            _resident((V_DIM, D_MODEL)),
            _resident((D_MODEL, D_MODEL)),
        ],
        out_specs=row(D_MODEL, 0),
        out_shape=jax.ShapeDtypeStruct((B, S, D_MODEL), F32),
        compiler_params=_params("parallel", "parallel"),
        name="merge",
    )(hf, hb, proj, proj, u, x, head_g, w_conv_out, w_mlstm_out, w_out)


def _ffn_kernel(x_ref, prev_ref, next_ref, g_ref, wg_ref, wv_ref, cw_ref, cb_ref, wd_ref, gf_ref,
                out_ref, xn_ref, *, tm, final_norm):
    i, j = pl.program_id(1), pl.program_id(2)
    halo = FFN_HALO

    @pl.when(j == 0)
    def _():
        g = g_ref[...]
        xn_ref[0:tm, :] = _rms(x_ref[0], g).astype(BF16)
        edge = jnp.concatenate([
            jnp.where(i > 0, _rms(prev_ref[0], g), 0.0),
            jnp.where(i < pl.num_programs(1) - 1, _rms(next_ref[0], g), 0.0)], 0)
        xn_ref[tm:tm + 2 * halo, :] = edge.astype(BF16)
        out_ref[0] = x_ref[0]

    gate_all = jnp.dot(xn_ref[...], wg_ref[...], preferred_element_type=F32)
    val = jnp.dot(xn_ref[0:tm, :], wv_ref[...], preferred_element_type=F32)
    gate = gate_all[0:tm]
    before = gate_all[tm + halo - 1:tm + halo]
    after = gate_all[tm + halo:tm + halo + 1]
    row_id = lax.broadcasted_iota(jnp.int32, (SUBLANES, gate.shape[1]), 0)
    down = pltpu.roll(gate, 1, 0)
    down = jnp.concatenate([jnp.where(row_id == 0, before, down[0:SUBLANES]), down[SUBLANES:]], 0)
    up = pltpu.roll(gate, tm - 1, 0)
    up = jnp.concatenate([up[:tm - SUBLANES], jnp.where(row_id == SUBLANES - 1, after, up[tm - SUBLANES:])], 0)
    conv = cw_ref[0:1, :] * down + cw_ref[1:2, :] * gate + cw_ref[2:3, :] * up + cb_ref[...]
    hid = 0.5 * conv * (1.0 + lax.erf(conv * (2.0 ** -0.5))) * val
    out_ref[0] += jnp.dot(hid.astype(BF16), wd_ref[...], preferred_element_type=F32)

    if final_norm:
        @pl.when(j == pl.num_programs(2) - 1)
        def _():
            out_ref[0] = _rms(out_ref[0], gf_ref[...])


def _ffn(x, g, w_gate, w_val, conv_w, conv_b, w_down, g_final, final_norm):
    B, S, _ = x.shape
    tm = min(FFN_TM, S)
    halo = FFN_HALO
    per = tm // halo
    last = S // halo - 1
    return pl.pallas_call(
        functools.partial(_ffn_kernel, tm=tm, final_norm=final_norm),
        grid=(B, S // tm, D_FF_PAD // FFN_TF),
        in_specs=[
            pl.BlockSpec((1, tm, D_MODEL), lambda b, i, j: (b, i, 0)),
            pl.BlockSpec((1, halo, D_MODEL), lambda b, i, j: (b, jnp.maximum(i * per - 1, 0), 0)),
            pl.BlockSpec((1, halo, D_MODEL), lambda b, i, j: (b, jnp.minimum((i + 1) * per, last), 0)),
            pl.BlockSpec((1, D_MODEL), lambda b, i, j: (0, 0)),
            pl.BlockSpec((D_MODEL, FFN_TF), lambda b, i, j: (0, j)),
            pl.BlockSpec((D_MODEL, FFN_TF), lambda b, i, j: (0, j)),
            pl.BlockSpec((SUBLANES, FFN_TF), lambda b, i, j: (0, j)),
            pl.BlockSpec((1, FFN_TF), lambda b, i, j: (0, j)),
            pl.BlockSpec((FFN_TF, D_MODEL), lambda b, i, j: (j, 0)),
            pl.BlockSpec((1, D_MODEL), lambda b, i, j: (0, 0)),
        ],
        out_specs=pl.BlockSpec((1, tm, D_MODEL), lambda b, i, j: (b, i, 0)),
        out_shape=jax.ShapeDtypeStruct((B, S, D_MODEL), F32),
        scratch_shapes=[pltpu.VMEM((tm + 2 * halo, D_MODEL), BF16)],
        compiler_params=_params("parallel", "parallel", "arbitrary"),
        name="ffn",
    )(x, x, x, g, w_gate, w_val, conv_w, conv_b, w_down, g_final)


def _pad_to(a, axis, size):
    pad = [(0, 0)] * a.ndim
    pad[axis] = (0, size - a.shape[axis])
    return jnp.pad(a, pad)


def _cell_gates(a):
    g = a.reshape(a.shape[:-1] + (4, HEADS))
    ig = g[..., 0::2, :].reshape(a.shape[:-1] + (CELLS,))
    lf = g[..., 1::2, :].reshape(a.shape[:-1] + (CELLS,))
    return jnp.concatenate([_pad_to(ig, -1 % a.ndim, LANES), _pad_to(lf, -1 % a.ndim, LANES)], -1)


def _layer_params(l, norm_mix_g, w_in, b_gates, conv_dw_w, conv_dw_b, conv_ln_g, conv_ln_b, w_conv_out,
                  mlstm_head_g, w_mlstm_out, w_out, norm_ffn_g, w_up, ffn_dw_w, ffn_dw_b, w_down):
    wi = w_in[l]
    row = lambda a: a[l].reshape(1, -1)
    return dict(
        norm_mix_g=row(norm_mix_g),
        w_main=jnp.concatenate([wi[:, :GATE_OFF], wi[:, GATE_OFF + N_GATES:]], 1).astype(BF16),
        w_gate=_cell_gates(wi[:, GATE_OFF:GATE_OFF + N_GATES]).astype(BF16),
        b_gates=_cell_gates(row(b_gates)),
        conv_w=_pad_to(conv_dw_w[l], 0, CONV_WIDTH + 1),
        conv_b=row(conv_dw_b), ln_g=row(conv_ln_g), ln_b=row(conv_ln_b),
        w_conv_out=w_conv_out[l].astype(BF16),
        head_g=row(mlstm_head_g),
        w_mlstm_out=w_mlstm_out[l].astype(BF16),
        w_out=w_out[l].astype(BF16),
        norm_ffn_g=row(norm_ffn_g),
        w_up_gate=_pad_to(w_up[l][:, :D_FF], 1, D_FF_PAD).astype(BF16),
        w_up_val=_pad_to(w_up[l][:, D_FF:], 1, D_FF_PAD).astype(BF16),
        ffn_w=_pad_to(_pad_to(ffn_dw_w[l], 0, SUBLANES), 1, D_FF_PAD),
        ffn_b=_pad_to(row(ffn_dw_b), 1, D_FF_PAD),
        w_down=_pad_to(w_down[l], 0, D_FF_PAD).astype(BF16),
    )


def _trunk(x, layers, g_final):
    for l, p in enumerate(layers):
        proj, gates = _in_proj(x, p["norm_mix_g"], p["w_main"], p["w_gate"])
        tab, row_c, dec = _gate_scan(gates, p["b_gates"])
        u = _conv_branch(proj, p["conv_w"], p["conv_b"], p["ln_g"], p["ln_b"])
        hf, hb = _mlstm(proj, tab, row_c, dec)
        x = _merge(hf, hb, proj, u, x, p["head_g"], p["w_conv_out"], p["w_mlstm_out"], p["w_out"])
        x = _ffn(x, p["norm_ffn_g"], p["w_up_gate"], p["w_up_val"], p["ffn_w"], p["ffn_b"], p["w_down"],
                 g_final, final_norm=(l == len(layers) - 1))
    return x


def kernel(x_prompt, x_sample, norm_mix_g, w_in, b_gates, conv_dw_w, conv_dw_b, conv_ln_g, conv_ln_b,
           w_conv_out, mlstm_head_g, w_mlstm_out, w_out, norm_ffn_g, w_up, ffn_dw_w, ffn_dw_b, w_down,
           norm_final_g):
    depth = w_in.shape[0]
    layers = [
        _layer_params(l, norm_mix_g, w_in, b_gates, conv_dw_w, conv_dw_b, conv_ln_g, conv_ln_b, w_conv_out,
                      mlstm_head_g, w_mlstm_out, w_out, norm_ffn_g, w_up, ffn_dw_w, ffn_dw_b, w_down)
        for l in range(depth)
    ]
    g_final = norm_final_g.reshape(1, -1)
    return (_trunk(x_prompt, layers, g_final), _trunk(x_sample, layers, g_final))
```

```python
import functools

import jax
import jax.numpy as jnp
from jax import lax
from jax.experimental import pallas as pl
from jax.experimental.pallas import tpu as pltpu

F32 = jnp.float32
BF16 = jnp.bfloat16

D_MODEL = 2048
CONV_DIM = 1024
CONV_WIDTH = 31
CONV_PAD = (CONV_WIDTH - 1) // 2
HEADS = 4
CELLS = 2 * HEADS
DK = 256
DV = 512
QK_DIM = HEADS * DK
V_DIM = HEADS * DV
CHUNK = 256
D_FF = 5504
FFN_CONV_WIDTH = 3
EPS = 1e-6

LANES = 128
SUBLANES = 8
BF16_SUBLANES = 16
VMEM_LIMIT = 56 * 1024 * 1024

COL_CONV = 0
COL_Q = 2 * CONV_DIM
COL_K = COL_Q + QK_DIM
COL_V = COL_K + QK_DIM
COL_O = COL_V + V_DIM
COL_MERGE = COL_O + V_DIM
D_MAIN = COL_MERGE + 2 * D_MODEL
N_GATES = 4 * HEADS
GATE_OFF = COL_O + V_DIM
GATE_W = 2 * LANES

G_ALPHA, G_INTER, G_FLOOR, G_KSCALE, G_GROUPS = 0, 1, 2, 3, 4

IN_TM, IN_TN = 1024, 2048
SCAN_CHUNKS = 4
CONV_TM, CONV_ROWS = 512, 32
CONV_HALO = BF16_SUBLANES
MERGE_TM = 256
FFN_TM, FFN_TF = 1024, 512
D_FF_PAD = -(-D_FF // FFN_TF) * FFN_TF
FFN_HALO = SUBLANES


def _params(*sem):
    return pltpu.CompilerParams(dimension_semantics=sem, vmem_limit_bytes=VMEM_LIMIT)


def _resident(shape):
    return pl.BlockSpec(shape, lambda *_: (0,) * len(shape), pipeline_mode=pl.Buffered(1))


def _rms(x, g):
    return x * lax.rsqrt(jnp.mean(x * x, -1, keepdims=True) + EPS) * g


def _in_proj_kernel(x_ref, g_ref, w_ref, wg_ref, proj_ref, gates_ref, xn_ref):
    @pl.when(pl.program_id(2) == 0)
    def _():
        xn = _rms(x_ref[0], g_ref[...]).astype(BF16)
        xn_ref[...] = xn
        gates_ref[0] = jnp.dot(xn, wg_ref[...], preferred_element_type=F32)

    proj_ref[0] = jnp.dot(xn_ref[...], w_ref[...], preferred_element_type=F32).astype(BF16)


def _in_proj(x, g, w_main, w_gate):
    B, S, _ = x.shape
    tm = min(IN_TM, S)
    return pl.pallas_call(
        _in_proj_kernel,
        grid=(B, S // tm, D_MAIN // IN_TN),
        in_specs=[
            pl.BlockSpec((1, tm, D_MODEL), lambda b, i, j: (b, i, 0)),
            pl.BlockSpec((1, D_MODEL), lambda b, i, j: (0, 0)),
            pl.BlockSpec((D_MODEL, IN_TN), lambda b, i, j: (0, j)),
            pl.BlockSpec((D_MODEL, GATE_W), lambda b, i, j: (0, 0)),
        ],
        out_specs=[
            pl.BlockSpec((1, tm, IN_TN), lambda b, i, j: (b, i, j)),
            pl.BlockSpec((1, tm, GATE_W), lambda b, i, j: (b, i, 0)),
        ],
        out_shape=[
            jax.ShapeDtypeStruct((B, S, D_MAIN), BF16),
            jax.ShapeDtypeStruct((B, S, GATE_W), F32),
        ],
        scratch_shapes=[pltpu.VMEM((tm, D_MODEL), BF16)],
        compiler_params=_params("parallel", "parallel", "arbitrary"),
        name="in_proj",
    )(x, g, w_main, w_gate)


def _log_sigmoid(x):
    return jnp.minimum(x, 0.0) - jnp.log1p(jnp.exp(-jnp.abs(x)))


def _split3(x):
    hi = x.astype(BF16)
    r = x - hi.astype(F32)
    mid = r.astype(BF16)
    lo = (r - mid.astype(F32)).astype(BF16)
    return hi, mid, lo


def _gate_scan_kernel(rawf_ref, rawb_ref, bias_ref, tab_ref, row_ref, dec_ref, m_ref):
    @pl.when(pl.program_id(1) == 0)
    def _():
        m_ref[...] = jnp.zeros_like(m_ref)

    n = CHUNK
    lane = lax.broadcasted_iota(jnp.int32, (n, LANES), 1)
    row_id = lax.broadcasted_iota(jnp.int32, (n, LANES), 0)
    is_f = lane < HEADS
    r = lax.broadcasted_iota(jnp.int32, (n, n), 0)
    c = lax.broadcasted_iota(jnp.int32, (n, n), 1)
    tril = (c <= r).astype(BF16)
    triu = (c >= r).astype(BF16)
    m = m_ref[...]
    for i in range(SCAN_CHUNKS):
        lo, hi = i * n, (i + 1) * n
        rf = rawf_ref[0, lo:hi, :]
        rb = rawb_ref[0, (SCAN_CHUNKS - 1 - i) * n:(SCAN_CHUNKS - i) * n, :]
        ig = jnp.where(is_f, rf[:, :LANES], rb[:, :LANES]) + bias_ref[:, :LANES]
        lf = _log_sigmoid(jnp.where(is_f, rf[:, LANES:], rb[:, LANES:]) + bias_ref[:, LANES:])
        cs_f = jnp.zeros_like(lf)
        cs_b = jnp.zeros_like(lf)
        for part in _split3(lf):
            cs_f = cs_f + jnp.dot(tril, part, preferred_element_type=F32)
            cs_b = cs_b + jnp.dot(triu, part, preferred_element_type=F32)
        b = jnp.where(is_f, cs_f, cs_b)
        cc = ig - b
        cm = cc
        k = 1
        while k < n:
            dn = jnp.where(row_id >= k, pltpu.roll(cm, k, 0), -jnp.inf)
            up = jnp.where(row_id < n - k, pltpu.roll(cm, n - k, 0), -jnp.inf)
            cm = jnp.maximum(cm, jnp.where(is_f, dn, up))
            k *= 2
        mm = jnp.maximum(m, cm)
        total = jnp.where(is_f[:1], b[n - 1:n], b[0:1])
        m_end = jnp.where(is_f[:1], mm[n - 1:n], mm[0:1])
        tab_ref[0, lo:hi, G_ALPHA * LANES:(G_ALPHA + 1) * LANES] = -mm
        tab_ref[0, lo:hi, G_INTER * LANES:(G_INTER + 1) * LANES] = jnp.exp(m - mm)
        tab_ref[0, lo:hi, G_FLOOR * LANES:(G_FLOOR + 1) * LANES] = jnp.exp(-(b + mm))
        tab_ref[0, lo:hi, G_KSCALE * LANES:(G_KSCALE + 1) * LANES] = jnp.exp(cc - m_end)
        row_ref[0, :, lo:hi] = cc.T[:CELLS, :]
        dec_ref[0, i] = jnp.exp(m - m_end)
        m = total + m_end
    m_ref[...] = m


def _gate_scan(raw, bias):
    B, S, _ = raw.shape
    rows = SCAN_CHUNKS * CHUNK
    nj = S // rows
    return pl.pallas_call(
        _gate_scan_kernel,
        grid=(B, nj),
        in_specs=[
            pl.BlockSpec((1, rows, GATE_W), lambda b, j: (b, j, 0)),
            pl.BlockSpec((1, rows, GATE_W), lambda b, j: (b, nj - 1 - j, 0)),
            pl.BlockSpec((1, GATE_W), lambda b, j: (0, 0)),
        ],
        out_specs=[
            pl.BlockSpec((1, rows, G_GROUPS * LANES), lambda b, j: (b, j, 0)),
            pl.BlockSpec((1, CELLS, rows), lambda b, j: (b, 0, j)),
            pl.BlockSpec((1, SCAN_CHUNKS, 1, LANES), lambda b, j: (b, j, 0, 0)),
        ],
        out_shape=[
            jax.ShapeDtypeStruct((B, S, G_GROUPS * LANES), F32),
            jax.ShapeDtypeStruct((B, CELLS, S), F32),
            jax.ShapeDtypeStruct((B, S // CHUNK, 1, LANES), F32),
        ],
        scratch_shapes=[pltpu.VMEM((1, LANES), F32)],
        compiler_params=_params("parallel", "arbitrary"),
        name="gate_scan",
    )(raw, raw, bias)


def _conv_kernel(main_ref, prev_ref, next_ref, w_ref, b_ref, lg_ref, lb_ref, out_ref, u_ref, *, tm):
    i = pl.program_id(1)
    halo = CONV_HALO
    span = tm + 2 * halo - SUBLANES

    def glu(blk):
        return blk[:, :CONV_DIM].astype(F32) * jax.nn.sigmoid(blk[:, CONV_DIM:].astype(F32))

    u_ref[0, 0:halo, :] = jnp.where(i > 0, glu(prev_ref[0]), 0.0)
    u_ref[0, halo:halo + tm, :] = glu(main_ref[0])
    u_ref[0, halo + tm:2 * halo + tm, :] = jnp.where(i < pl.num_programs(1) - 1, glu(next_ref[0]), 0.0)
    for s in range(1, SUBLANES):
        u_ref[s, 0:span, :] = u_ref[0, s:s + span, :]

    grp = CONV_ROWS // SUBLANES
    for r0 in range(0, tm, CONV_ROWS):
        acc = jnp.broadcast_to(b_ref[...].reshape(1, 1, CONV_DIM), (grp, SUBLANES, CONV_DIM))
        for j in range(CONV_WIDTH):
            off = r0 + halo - CONV_PAD + j
            lo = off - off % SUBLANES
            blk = u_ref[off % SUBLANES, lo:lo + CONV_ROWS, :].reshape(grp, SUBLANES, CONV_DIM)
            acc = acc + w_ref[j][None] * blk
        acc = acc.reshape(CONV_ROWS, CONV_DIM)
        mu = jnp.mean(acc, -1, keepdims=True)
        xc = acc - mu
        y = xc * lax.rsqrt(jnp.mean(xc * xc, -1, keepdims=True) + EPS) * lg_ref[...] + lb_ref[...]
        out_ref[0, r0:r0 + CONV_ROWS, :] = (y * jax.nn.sigmoid(y)).astype(BF16)


def _conv_branch(proj, w, b, lg, lb):
    B, S, _ = proj.shape
    tm = min(CONV_TM, S)
    halo = CONV_HALO
    per = tm // halo
    last = S // halo - 1
    return pl.pallas_call(
        functools.partial(_conv_kernel, tm=tm),
        grid=(B, S // tm),
        in_specs=[
            pl.BlockSpec((1, tm, 2 * CONV_DIM), lambda b, i: (b, i, 0)),
            pl.BlockSpec((1, halo, 2 * CONV_DIM), lambda b, i: (b, jnp.maximum(i * per - 1, 0), 0)),
            pl.BlockSpec((1, halo, 2 * CONV_DIM), lambda b, i: (b, jnp.minimum((i + 1) * per, last), 0)),
            pl.BlockSpec((CONV_WIDTH, SUBLANES, CONV_DIM), lambda b, i: (0, 0, 0)),
            pl.BlockSpec((1, CONV_DIM), lambda b, i: (0, 0)),
            pl.BlockSpec((1, CONV_DIM), lambda b, i: (0, 0)),
            pl.BlockSpec((1, CONV_DIM), lambda b, i: (0, 0)),
        ],
        out_specs=pl.BlockSpec((1, tm, CONV_DIM), lambda b, i: (b, i, 0)),
        out_shape=jax.ShapeDtypeStruct((B, S, CONV_DIM), BF16),
        scratch_shapes=[pltpu.VMEM((SUBLANES, tm + 2 * halo, CONV_DIM), F32)],
        compiler_params=_params("parallel", "parallel"),
        name="conv_branch",
    )(proj, proj, proj, w, b, lg, lb)


def _mlstm_cell(q, k, v, alpha, c_row, inter, floor, kscale, decay, mask, c_ref, n_ref, idx):
    scale = DK ** -0.5
    c_state = c_ref[idx]
    n_state = n_ref[idx]
    w = jnp.exp(jnp.where(mask, alpha + c_row, -jnp.inf))
    qk = lax.dot_general(q, k, (((1,), (1,)), ((), ())), preferred_element_type=F32)
    s = qk * scale * w
    q_c = jnp.dot(q, c_state.astype(BF16), preferred_element_type=F32) * scale
    wk = kscale * k.astype(F32)
    both = jnp.dot(jnp.concatenate([s.astype(BF16), wk.astype(BF16).T], 0), v, preferred_element_type=F32)
    num = inter * q_c + both[:CHUNK]
    q_n = jnp.sum(q.astype(F32) * n_state, -1, keepdims=True) * scale
    den = inter * q_n + jnp.sum(s, -1, keepdims=True)
    h = num * (1.0 / jnp.maximum(jnp.abs(den), floor))
    c_ref[idx] = decay * c_state + both[CHUNK:]
    n_ref[idx] = decay * n_state + jnp.sum(wk, 0, keepdims=True)
    return h


def _mlstm_kernel(qf_ref, kf_ref, vf_ref, qb_ref, kb_ref, vb_ref, tab_ref, row_ref, dec_ref,
                  hf_ref, hb_ref, c_ref, n_ref):
    @pl.when(pl.program_id(1) == 0)
    def _():
        c_ref[...] = jnp.zeros_like(c_ref)
        n_ref[...] = jnp.zeros_like(n_ref)

    r = lax.broadcasted_iota(jnp.int32, (CHUNK, CHUNK), 0)
    c = lax.broadcasted_iota(jnp.int32, (CHUNK, CHUNK), 1)
    dirs = ((qf_ref, kf_ref, vf_ref, hf_ref, c <= r), (qb_ref, kb_ref, vb_ref, hb_ref, c >= r))
    decay = dec_ref[0, 0]
    for d, (q_ref, k_ref, v_ref, h_ref, mask) in enumerate(dirs):
        for h in range(HEADS):
            cell = d * HEADS + h
            col = lambda grp: tab_ref[0, :, grp * LANES + cell:grp * LANES + cell + 1]
            out = _mlstm_cell(
                q_ref[0, :, h * DK:(h + 1) * DK], k_ref[0, :, h * DK:(h + 1) * DK],
                v_ref[0, :, h * DV:(h + 1) * DV],
                col(G_ALPHA), row_ref[0, cell:cell + 1, :], col(G_INTER), col(G_FLOOR), col(G_KSCALE),
                decay[:, cell:cell + 1], mask, c_ref, n_ref, cell)
            h_ref[0, :, h * DV:(h + 1) * DV] = out.astype(BF16)


def _mlstm(proj, tab, row_c, dec):
    B, S, _ = proj.shape
    nc = S // CHUNK
    fwd = lambda blk: (lambda b, c: (b, c, blk))
    bwd = lambda blk: (lambda b, c: (b, nc - 1 - c, blk))
    qkv = lambda im: [
        pl.BlockSpec((1, CHUNK, QK_DIM), im(COL_Q // QK_DIM)),
        pl.BlockSpec((1, CHUNK, QK_DIM), im(COL_K // QK_DIM)),
        pl.BlockSpec((1, CHUNK, V_DIM), im(COL_V // V_DIM)),
    ]
    return pl.pallas_call(
        _mlstm_kernel,
        grid=(B, nc),
        in_specs=qkv(fwd) + qkv(bwd) + [
            pl.BlockSpec((1, CHUNK, G_GROUPS * LANES), fwd(0)),
            pl.BlockSpec((1, CELLS, CHUNK), lambda b, c: (b, 0, c)),
            pl.BlockSpec((1, 1, 1, LANES), lambda b, c: (b, c, 0, 0)),
        ],
        out_specs=[
            pl.BlockSpec((1, CHUNK, V_DIM), fwd(0)),
            pl.BlockSpec((1, CHUNK, V_DIM), bwd(0)),
        ],
        out_shape=[jax.ShapeDtypeStruct((B, S, V_DIM), BF16)] * 2,
        scratch_shapes=[
            pltpu.VMEM((CELLS, DK, DV), F32),
            pltpu.VMEM((CELLS, 1, DK), F32),
        ],
        compiler_params=_params("parallel", "arbitrary"),
        name="mlstm",
    )(proj, proj, proj, proj, proj, proj, tab, row_c, dec)


def _merge_kernel(hf_ref, hb_ref, o_ref, mg_ref, u_ref, x_ref, hg_ref, wc_ref, wm_ref, wo_ref, out_ref):
    hs = hf_ref[0].astype(F32) + hb_ref[0].astype(F32)
    heads = []
    for h in range(HEADS):
        blk = hs[:, h * DV:(h + 1) * DV]
        heads.append(_rms(blk, hg_ref[:, h * DV:(h + 1) * DV]))
    hn = jnp.concatenate(heads, -1) * jax.nn.sigmoid(o_ref[0].astype(F32))
    y_mlstm = jnp.dot(hn.astype(BF16), wm_ref[...], preferred_element_type=F32)
    y_conv = jnp.dot(u_ref[0], wc_ref[...], preferred_element_type=F32)
    mg = mg_ref[0]
    mixed = (jax.nn.sigmoid(mg[:, :D_MODEL].astype(F32)) * y_conv
             + jax.nn.sigmoid(mg[:, D_MODEL:].astype(F32)) * y_mlstm)
    out_ref[0] = x_ref[0] + jnp.dot(mixed.astype(BF16), wo_ref[...], preferred_element_type=F32)


def _merge(hf, hb, proj, u, x, head_g, w_conv_out, w_mlstm_out, w_out):
    B, S, _ = x.shape
    tm = min(MERGE_TM, S)
    row = lambda width, blk: pl.BlockSpec((1, tm, width), lambda b, i: (b, i, blk))
    return pl.pallas_call(
        _merge_kernel,
        grid=(B, S // tm),
        in_specs=[
            row(V_DIM, 0), row(V_DIM, 0),
            row(V_DIM, COL_O // V_DIM),
            row(2 * D_MODEL, COL_MERGE // (2 * D_MODEL)),
            row(CONV_DIM, 0),
            row(D_MODEL, 0),
            _resident((1, V_DIM)),
            _resident((CONV_DIM, D_MODEL)),
            _resident((V_DIM, D_MODEL)),
            _resident((D_MODEL, D_MODEL)),
        ],
        out_specs=row(D_MODEL, 0),
        out_shape=jax.ShapeDtypeStruct((B, S, D_MODEL), F32),
        compiler_params=_params("parallel", "parallel"),
        name="merge",
    )(hf, hb, proj, proj, u, x, head_g, w_conv_out, w_mlstm_out, w_out)


def _ffn_kernel(x_ref, prev_ref, next_ref, g_ref, wg_ref, wv_ref, cw_ref, cb_ref, wd_ref, gf_ref,
                out_ref, xn_ref, *, tm, final_norm):
    i, j = pl.program_id(1), pl.program_id(2)
    halo = FFN_HALO

    @pl.when(j == 0)
    def _():
        g = g_ref[...]
        xn_ref[0:tm, :] = _rms(x_ref[0], g).astype(BF16)
        edge = jnp.concatenate([
            jnp.where(i > 0, _rms(prev_ref[0], g), 0.0),
            jnp.where(i < pl.num_programs(1) - 1, _rms(next_ref[0], g), 0.0)], 0)
        xn_ref[tm:tm + 2 * halo, :] = edge.astype(BF16)
        out_ref[0] = x_ref[0]

    gate_all = jnp.dot(xn_ref[...], wg_ref[...], preferred_element_type=F32)
    val = jnp.dot(xn_ref[0:tm, :], wv_ref[...], preferred_element_type=F32)
    gate = gate_all[0:tm]
    before = gate_all[tm + halo - 1:tm + halo]
    after = gate_all[tm + halo:tm + halo + 1]
    row_id = lax.broadcasted_iota(jnp.int32, (SUBLANES, gate.shape[1]), 0)
    down = pltpu.roll(gate, 1, 0)
    down = jnp.concatenate([jnp.where(row_id == 0, before, down[0:SUBLANES]), down[SUBLANES:]], 0)
    up = pltpu.roll(gate, tm - 1, 0)
    up = jnp.concatenate([up[:tm - SUBLANES], jnp.where(row_id == SUBLANES - 1, after, up[tm - SUBLANES:])], 0)
    conv = cw_ref[0:1, :] * down + cw_ref[1:2, :] * gate + cw_ref[2:3, :] * up + cb_ref[...]
    hid = 0.5 * conv * (1.0 + lax.erf(conv * (2.0 ** -0.5))) * val
    out_ref[0] += jnp.dot(hid.astype(BF16), wd_ref[...], preferred_element_type=F32)

    if final_norm:
        @pl.when(j == pl.num_programs(2) - 1)
        def _():
            out_ref[0] = _rms(out_ref[0], gf_ref[...])


def _ffn(x, g, w_gate, w_val, conv_w, conv_b, w_down, g_final, final_norm):
    B, S, _ = x.shape
    tm = min(FFN_TM, S)
    halo = FFN_HALO
    per = tm // halo
    last = S // halo - 1
    return pl.pallas_call(
        functools.partial(_ffn_kernel, tm=tm, final_norm=final_norm),
        grid=(B, S // tm, D_FF_PAD // FFN_TF),
        in_specs=[
            pl.BlockSpec((1, tm, D_MODEL), lambda b, i, j: (b, i, 0), pipeline_mode=pl.Buffered(1)),
            pl.BlockSpec((1, halo, D_MODEL), lambda b, i, j: (b, jnp.maximum(i * per - 1, 0), 0)),
            pl.BlockSpec((1, halo, D_MODEL), lambda b, i, j: (b, jnp.minimum((i + 1) * per, last), 0)),
            pl.BlockSpec((1, D_MODEL), lambda b, i, j: (0, 0)),
            pl.BlockSpec((D_MODEL, FFN_TF), lambda b, i, j: (0, j)),
            pl.BlockSpec((D_MODEL, FFN_TF), lambda b, i, j: (0, j)),
            pl.BlockSpec((SUBLANES, FFN_TF), lambda b, i, j: (0, j)),
            pl.BlockSpec((1, FFN_TF), lambda b, i, j: (0, j)),
            pl.BlockSpec((FFN_TF, D_MODEL), lambda b, i, j: (j, 0)),
            pl.BlockSpec((1, D_MODEL), lambda b, i, j: (0, 0)),
        ],
        out_specs=pl.BlockSpec((1, tm, D_MODEL), lambda b, i, j: (b, i, 0)),
        out_shape=jax.ShapeDtypeStruct((B, S, D_MODEL), F32),
        scratch_shapes=[pltpu.VMEM((tm + 2 * halo, D_MODEL), BF16)],
        compiler_params=_params("parallel", "parallel", "arbitrary"),
        name="ffn",
    )(x, x, x, g, w_gate, w_val, conv_w, conv_b, w_down, g_final)


def _pad_to(a, axis, size):
    pad = [(0, 0)] * a.ndim
    pad[axis] = (0, size - a.shape[axis])
    return jnp.pad(a, pad)


def _cell_gates(a):
    g = a.reshape(a.shape[:-1] + (4, HEADS))
    ig = g[..., 0::2, :].reshape(a.shape[:-1] + (CELLS,))
    lf = g[..., 1::2, :].reshape(a.shape[:-1] + (CELLS,))
    return jnp.concatenate([_pad_to(ig, -1 % a.ndim, LANES), _pad_to(lf, -1 % a.ndim, LANES)], -1)


def _layer_params(l, norm_mix_g, w_in, b_gates, conv_dw_w, conv_dw_b, conv_ln_g, conv_ln_b, w_conv_out,
                  mlstm_head_g, w_mlstm_out, w_out, norm_ffn_g, w_up, ffn_dw_w, ffn_dw_b, w_down):
    wi = w_in[l]
    row = lambda a: a[l].reshape(1, -1)
    return dict(
        norm_mix_g=row(norm_mix_g),
        w_main=jnp.concatenate([wi[:, :GATE_OFF], wi[:, GATE_OFF + N_GATES:]], 1).astype(BF16),
        w_gate=_cell_gates(wi[:, GATE_OFF:GATE_OFF + N_GATES]).astype(BF16),
        b_gates=_cell_gates(row(b_gates)),
        conv_w=jnp.broadcast_to(conv_dw_w[l][:, None, :], (CONV_WIDTH, SUBLANES, CONV_DIM)),
        conv_b=row(conv_dw_b), ln_g=row(conv_ln_g), ln_b=row(conv_ln_b),
        w_conv_out=w_conv_out[l].astype(BF16),
        head_g=row(mlstm_head_g),
        w_mlstm_out=w_mlstm_out[l].astype(BF16),
        w_out=w_out[l].astype(BF16),
        norm_ffn_g=row(norm_ffn_g),
        w_up_gate=_pad_to(w_up[l][:, :D_FF], 1, D_FF_PAD).astype(BF16),
        w_up_val=_pad_to(w_up[l][:, D_FF:], 1, D_FF_PAD).astype(BF16),
        ffn_w=_pad_to(_pad_to(ffn_dw_w[l], 0, SUBLANES), 1, D_FF_PAD),
        ffn_b=_pad_to(row(ffn_dw_b), 1, D_FF_PAD),
        w_down=_pad_to(w_down[l], 0, D_FF_PAD).astype(BF16),
    )


def _trunk(x, layers, g_final):
    for l, p in enumerate(layers):
        proj, gates = _in_proj(x, p["norm_mix_g"], p["w_main"], p["w_gate"])
        tab, row_c, dec = _gate_scan(gates, p["b_gates"])
        u = _conv_branch(proj, p["conv_w"], p["conv_b"], p["ln_g"], p["ln_b"])
        hf, hb = _mlstm(proj, tab, row_c, dec)
        x = _merge(hf, hb, proj, u, x, p["head_g"], p["w_conv_out"], p["w_mlstm_out"], p["w_out"])
        x = _ffn(x, p["norm_ffn_g"], p["w_up_gate"], p["w_up_val"], p["ffn_w"], p["ffn_b"], p["w_down"],
                 g_final, final_norm=(l == len(layers) - 1))
    return x


def kernel(x_prompt, x_sample, norm_mix_g, w_in, b_gates, conv_dw_w, conv_dw_b, conv_ln_g, conv_ln_b,
           w_conv_out, mlstm_head_g, w_mlstm_out, w_out, norm_ffn_g, w_up, ffn_dw_w, ffn_dw_b, w_down,
           norm_final_g):
    depth = w_in.shape[0]
    layers = [
        _layer_params(l, norm_mix_g, w_in, b_gates, conv_dw_w, conv_dw_b, conv_ln_g, conv_ln_b, w_conv_out,
                      mlstm_head_g, w_mlstm_out, w_out, norm_ffn_g, w_up, ffn_dw_w, ffn_dw_b, w_down)
        for l in range(depth)
    ]
    g_final = norm_final_g.reshape(1, -1)
    return (_trunk(x_prompt, layers, g_final), _trunk(x_sample, layers, g_final))
```

```python
import functools

import jax
import jax.numpy as jnp
from jax import lax
from jax.experimental import pallas as pl
from jax.experimental.pallas import tpu as pltpu

F32 = jnp.float32
BF16 = jnp.bfloat16

D_MODEL = 2048
CONV_DIM = 1024
CONV_WIDTH = 31
CONV_PAD = (CONV_WIDTH - 1) // 2
HEADS = 4
CELLS = 2 * HEADS
DK = 256
DV = 512
QK_DIM = HEADS * DK
V_DIM = HEADS * DV
CHUNK = 256
D_FF = 5504
FFN_CONV_WIDTH = 3
EPS = 1e-6

LANES = 128
SUBLANES = 8
BF16_SUBLANES = 16
VMEM_LIMIT = 56 * 1024 * 1024

COL_CONV = 0
COL_Q = 2 * CONV_DIM
COL_K = COL_Q + QK_DIM
COL_V = COL_K + QK_DIM
COL_O = COL_V + V_DIM
COL_MERGE = COL_O + V_DIM
D_MAIN = COL_MERGE + 2 * D_MODEL
N_GATES = 4 * HEADS
GATE_OFF = COL_O + V_DIM
GATE_W = 2 * LANES

G_ALPHA, G_INTER, G_FLOOR, G_KSCALE, G_GROUPS = 0, 1, 2, 3, 4

IN_TM, IN_TN = 1024, 2048
SCAN_CHUNKS = 4
CONV_TM, CONV_ROWS = 512, 32
CONV_HALO = BF16_SUBLANES
MERGE_TM = 256
FFN_TM, FFN_TF = 1024, 512
D_FF_PAD = -(-D_FF // FFN_TF) * FFN_TF
FFN_HALO = SUBLANES


def _params(*sem):
    return pltpu.CompilerParams(dimension_semantics=sem, vmem_limit_bytes=VMEM_LIMIT)


def _resident(shape):
    return pl.BlockSpec(shape, lambda *_: (0,) * len(shape), pipeline_mode=pl.Buffered(1))


def _rms(x, g):
    return x * lax.rsqrt(jnp.mean(x * x, -1, keepdims=True) + EPS) * g


def _in_proj_kernel(x_ref, g_ref, w_ref, wg_ref, proj_ref, gates_ref, xn_ref):
    @pl.when(pl.program_id(2) == 0)
    def _():
        xn = _rms(x_ref[0], g_ref[...]).astype(BF16)
        xn_ref[...] = xn
        gates_ref[0] = jnp.dot(xn, wg_ref[...], preferred_element_type=F32)

    proj_ref[0] = lax.dot_general(xn_ref[...], w_ref[...], (((1,), (1,)), ((), ())),
                                  preferred_element_type=F32).astype(BF16)


def _in_proj(x, g, w_main, layer, w_gate):
    B, S, _ = x.shape
    tm = min(IN_TM, S)
    return pl.pallas_call(
        _in_proj_kernel,
        grid=(B, S // tm, D_MAIN // IN_TN),
        in_specs=[
            pl.BlockSpec((1, tm, D_MODEL), lambda b, i, j: (b, i, 0)),
            pl.BlockSpec((1, D_MODEL), lambda b, i, j: (0, 0)),
            pl.BlockSpec((None, IN_TN, D_MODEL), lambda b, i, j: (layer, j, 0)),
            pl.BlockSpec((D_MODEL, GATE_W), lambda b, i, j: (0, 0)),
        ],
        out_specs=[
            pl.BlockSpec((1, tm, IN_TN), lambda b, i, j: (b, i, j)),
            pl.BlockSpec((1, tm, GATE_W), lambda b, i, j: (b, i, 0)),
        ],
        out_shape=[
            jax.ShapeDtypeStruct((B, S, D_MAIN), BF16),
            jax.ShapeDtypeStruct((B, S, GATE_W), F32),
        ],
        scratch_shapes=[pltpu.VMEM((tm, D_MODEL), BF16)],
        compiler_params=_params("parallel", "parallel", "arbitrary"),
        name="in_proj",
    )(x, g, w_main, w_gate)


def _log_sigmoid(x):
    return jnp.minimum(x, 0.0) - jnp.log1p(jnp.exp(-jnp.abs(x)))


def _split3(x):
    hi = x.astype(BF16)
    r = x - hi.astype(F32)
    mid = r.astype(BF16)
    lo = (r - mid.astype(F32)).astype(BF16)
    return hi, mid, lo


def _gate_scan_kernel(rawf_ref, rawb_ref, bias_ref, tab_ref, row_ref, dec_ref, m_ref):
    @pl.when(pl.program_id(1) == 0)
    def _():
        m_ref[...] = jnp.zeros_like(m_ref)

    n = CHUNK
    lane = lax.broadcasted_iota(jnp.int32, (n, LANES), 1)
    row_id = lax.broadcasted_iota(jnp.int32, (n, LANES), 0)
    is_f = lane < HEADS
    r = lax.broadcasted_iota(jnp.int32, (n, n), 0)
    c = lax.broadcasted_iota(jnp.int32, (n, n), 1)
    tril = (c <= r).astype(BF16)
    triu = (c >= r).astype(BF16)
    m = m_ref[...]
    for i in range(SCAN_CHUNKS):
        lo, hi = i * n, (i + 1) * n
        rf = rawf_ref[0, lo:hi, :]
        rb = rawb_ref[0, (SCAN_CHUNKS - 1 - i) * n:(SCAN_CHUNKS - i) * n, :]
        ig = jnp.where(is_f, rf[:, :LANES], rb[:, :LANES]) + bias_ref[:, :LANES]
        lf = _log_sigmoid(jnp.where(is_f, rf[:, LANES:], rb[:, LANES:]) + bias_ref[:, LANES:])
        cs_f = jnp.zeros_like(lf)
        cs_b = jnp.zeros_like(lf)
        for part in _split3(lf):
            cs_f = cs_f + jnp.dot(tril, part, preferred_element_type=F32)
            cs_b = cs_b + jnp.dot(triu, part, preferred_element_type=F32)
        b = jnp.where(is_f, cs_f, cs_b)
        cc = ig - b
        cm = cc
        k = 1
        while k < n:
            dn = jnp.where(row_id >= k, pltpu.roll(cm, k, 0), -jnp.inf)
            up = jnp.where(row_id < n - k, pltpu.roll(cm, n - k, 0), -jnp.inf)
            cm = jnp.maximum(cm, jnp.where(is_f, dn, up))
            k *= 2
        mm = jnp.maximum(m, cm)
        total = jnp.where(is_f[:1], b[n - 1:n], b[0:1])
        m_end = jnp.where(is_f[:1], mm[n - 1:n], mm[0:1])
        tab_ref[0, lo:hi, G_ALPHA * LANES:(G_ALPHA + 1) * LANES] = -mm
        tab_ref[0, lo:hi, G_INTER * LANES:(G_INTER + 1) * LANES] = jnp.exp(m - mm)
        tab_ref[0, lo:hi, G_FLOOR * LANES:(G_FLOOR + 1) * LANES] = jnp.exp(-(b + mm))
        tab_ref[0, lo:hi, G_KSCALE * LANES:(G_KSCALE + 1) * LANES] = jnp.exp(cc - m_end)
        row_ref[0, :, lo:hi] = cc.T[:CELLS, :]
        dec_ref[0, i] = jnp.exp(m - m_end)
        m = total + m_end
    m_ref[...] = m


def _gate_scan(raw, bias):
    B, S, _ = raw.shape
    rows = SCAN_CHUNKS * CHUNK
    nj = S // rows
    return pl.pallas_call(
        _gate_scan_kernel,
        grid=(B, nj),
        in_specs=[
            pl.BlockSpec((1, rows, GATE_W), lambda b, j: (b, j, 0)),
            pl.BlockSpec((1, rows, GATE_W), lambda b, j: (b, nj - 1 - j, 0)),
            pl.BlockSpec((1, GATE_W), lambda b, j: (0, 0)),
        ],
        out_specs=[
            pl.BlockSpec((1, rows, G_GROUPS * LANES), lambda b, j: (b, j, 0)),
            pl.BlockSpec((1, CELLS, rows), lambda b, j: (b, 0, j)),
            pl.BlockSpec((1, SCAN_CHUNKS, 1, LANES), lambda b, j: (b, j, 0, 0)),
        ],
        out_shape=[
            jax.ShapeDtypeStruct((B, S, G_GROUPS * LANES), F32),
            jax.ShapeDtypeStruct((B, CELLS, S), F32),
            jax.ShapeDtypeStruct((B, S // CHUNK, 1, LANES), F32),
        ],
        scratch_shapes=[pltpu.VMEM((1, LANES), F32)],
        compiler_params=_params("parallel", "arbitrary"),
        name="gate_scan",
    )(raw, raw, bias)


def _conv_kernel(main_ref, prev_ref, next_ref, w_ref, b_ref, lg_ref, lb_ref, out_ref, u_ref, *, tm):
    i = pl.program_id(1)
    halo = CONV_HALO
    span = tm + 2 * halo - SUBLANES

    def glu(blk):
        return blk[:, :CONV_DIM].astype(F32) * jax.nn.sigmoid(blk[:, CONV_DIM:].astype(F32))

    u_ref[0, 0:halo, :] = jnp.where(i > 0, glu(prev_ref[0]), 0.0)
    u_ref[0, halo:halo + tm, :] = glu(main_ref[0])
    u_ref[0, halo + tm:2 * halo + tm, :] = jnp.where(i < pl.num_programs(1) - 1, glu(next_ref[0]), 0.0)
    for s in range(1, SUBLANES):
        u_ref[s, 0:span, :] = u_ref[0, s:s + span, :]

    grp = CONV_ROWS // SUBLANES
    for r0 in range(0, tm, CONV_ROWS):
        acc = jnp.broadcast_to(b_ref[...].reshape(1, 1, CONV_DIM), (grp, SUBLANES, CONV_DIM))
        for j in range(CONV_WIDTH):
            off = r0 + halo - CONV_PAD + j
            lo = off - off % SUBLANES
            blk = u_ref[off % SUBLANES, lo:lo + CONV_ROWS, :].reshape(grp, SUBLANES, CONV_DIM)
            acc = acc + w_ref[j][None] * blk
        acc = acc.reshape(CONV_ROWS, CONV_DIM)
        mu = jnp.mean(acc, -1, keepdims=True)
        xc = acc - mu
        y = xc * lax.rsqrt(jnp.mean(xc * xc, -1, keepdims=True) + EPS) * lg_ref[...] + lb_ref[...]
        out_ref[0, r0:r0 + CONV_ROWS, :] = (y * jax.nn.sigmoid(y)).astype(BF16)


def _conv_branch(proj, w, b, lg, lb):
    B, S, _ = proj.shape
    tm = min(CONV_TM, S)
    halo = CONV_HALO
    per = tm // halo
    last = S // halo - 1
    return pl.pallas_call(
        functools.partial(_conv_kernel, tm=tm),
        grid=(B, S // tm),
        in_specs=[
            pl.BlockSpec((1, tm, 2 * CONV_DIM), lambda b, i: (b, i, 0)),
            pl.BlockSpec((1, halo, 2 * CONV_DIM), lambda b, i: (b, jnp.maximum(i * per - 1, 0), 0)),
            pl.BlockSpec((1, halo, 2 * CONV_DIM), lambda b, i: (b, jnp.minimum((i + 1) * per, last), 0)),
            pl.BlockSpec((CONV_WIDTH, SUBLANES, CONV_DIM), lambda b, i: (0, 0, 0)),
            pl.BlockSpec((1, CONV_DIM), lambda b, i: (0, 0)),
            pl.BlockSpec((1, CONV_DIM), lambda b, i: (0, 0)),
            pl.BlockSpec((1, CONV_DIM), lambda b, i: (0, 0)),
        ],
        out_specs=pl.BlockSpec((1, tm, CONV_DIM), lambda b, i: (b, i, 0)),
        out_shape=jax.ShapeDtypeStruct((B, S, CONV_DIM), BF16),
        scratch_shapes=[pltpu.VMEM((SUBLANES, tm + 2 * halo, CONV_DIM), F32)],
        compiler_params=_params("parallel", "parallel"),
        name="conv_branch",
    )(proj, proj, proj, w, b, lg, lb)


def _mlstm_cell(q, k, v, alpha, c_row, inter, floor, kscale, decay, mask, c_ref, n_ref, idx):
    scale = DK ** -0.5
    c_state = c_ref[idx]
    n_state = n_ref[idx]
    w = jnp.exp(jnp.where(mask, alpha + c_row, -jnp.inf))
    qk = lax.dot_general(q, k, (((1,), (1,)), ((), ())), preferred_element_type=F32)
    s = qk * scale * w
    q_c = jnp.dot(q, c_state.astype(BF16), preferred_element_type=F32) * scale
    wk = kscale * k.astype(F32)
    both = jnp.dot(jnp.concatenate([s.astype(BF16), wk.astype(BF16).T], 0), v, preferred_element_type=F32)
    num = inter * q_c + both[:CHUNK]
    q_n = jnp.sum(q.astype(F32) * n_state, -1, keepdims=True) * scale
    den = inter * q_n + jnp.sum(s, -1, keepdims=True)
    h = num * (1.0 / jnp.maximum(jnp.abs(den), floor))
    c_ref[idx] = decay * c_state + both[CHUNK:]
    n_ref[idx] = decay * n_state + jnp.sum(wk, 0, keepdims=True)
    return h


def _mlstm_kernel(qf_ref, kf_ref, vf_ref, qb_ref, kb_ref, vb_ref, tab_ref, row_ref, dec_ref,
                  hf_ref, hb_ref, c_ref, n_ref):
    @pl.when(pl.program_id(1) == 0)
    def _():
        c_ref[...] = jnp.zeros_like(c_ref)
        n_ref[...] = jnp.zeros_like(n_ref)

    r = lax.broadcasted_iota(jnp.int32, (CHUNK, CHUNK), 0)
    c = lax.broadcasted_iota(jnp.int32, (CHUNK, CHUNK), 1)
    dirs = ((qf_ref, kf_ref, vf_ref, hf_ref, c <= r), (qb_ref, kb_ref, vb_ref, hb_ref, c >= r))
    decay = dec_ref[0, 0]
    for d, (q_ref, k_ref, v_ref, h_ref, mask) in enumerate(dirs):
        for h in range(HEADS):
            cell = d * HEADS + h
            col = lambda grp: tab_ref[0, :, grp * LANES + cell:grp * LANES + cell + 1]
            out = _mlstm_cell(
                q_ref[0, :, h * DK:(h + 1) * DK], k_ref[0, :, h * DK:(h + 1) * DK],
                v_ref[0, :, h * DV:(h + 1) * DV],
                col(G_ALPHA), row_ref[0, cell:cell + 1, :], col(G_INTER), col(G_FLOOR), col(G_KSCALE),
                decay[:, cell:cell + 1], mask, c_ref, n_ref, cell)
            h_ref[0, :, h * DV:(h + 1) * DV] = out.astype(BF16)


def _mlstm(proj, tab, row_c, dec):
    B, S, _ = proj.shape
    nc = S // CHUNK
    fwd = lambda blk: (lambda b, c: (b, c, blk))
    bwd = lambda blk: (lambda b, c: (b, nc - 1 - c, blk))
    qkv = lambda im: [
        pl.BlockSpec((1, CHUNK, QK_DIM), im(COL_Q // QK_DIM)),
        pl.BlockSpec((1, CHUNK, QK_DIM), im(COL_K // QK_DIM)),
        pl.BlockSpec((1, CHUNK, V_DIM), im(COL_V // V_DIM)),
    ]
    return pl.pallas_call(
        _mlstm_kernel,
        grid=(B, nc),
        in_specs=qkv(fwd) + qkv(bwd) + [
            pl.BlockSpec((1, CHUNK, G_GROUPS * LANES), fwd(0)),
            pl.BlockSpec((1, CELLS, CHUNK), lambda b, c: (b, 0, c)),
            pl.BlockSpec((1, 1, 1, LANES), lambda b, c: (b, c, 0, 0)),
        ],
        out_specs=[
            pl.BlockSpec((1, CHUNK, V_DIM), fwd(0)),
            pl.BlockSpec((1, CHUNK, V_DIM), bwd(0)),
        ],
        out_shape=[jax.ShapeDtypeStruct((B, S, V_DIM), BF16)] * 2,
        scratch_shapes=[
            pltpu.VMEM((CELLS, DK, DV), F32),
            pltpu.VMEM((CELLS, 1, DK), F32),
        ],
        compiler_params=_params("parallel", "arbitrary"),
        name="mlstm",
    )(proj, proj, proj, proj, proj, proj, tab, row_c, dec)


def _merge_kernel(hf_ref, hb_ref, o_ref, mg_ref, u_ref, x_ref, hg_ref, wc_ref, wm_ref, wo_ref, out_ref):
    hs = hf_ref[0].astype(F32) + hb_ref[0].astype(F32)
    heads = []
    for h in range(HEADS):
        blk = hs[:, h * DV:(h + 1) * DV]
        heads.append(_rms(blk, hg_ref[:, h * DV:(h + 1) * DV]))
    hn = jnp.concatenate(heads, -1) * jax.nn.sigmoid(o_ref[0].astype(F32))
    y_mlstm = jnp.dot(hn.astype(BF16), wm_ref[...], preferred_element_type=F32)
    y_conv = jnp.dot(u_ref[0], wc_ref[...], preferred_element_type=F32)
    mg = mg_ref[0]
    mixed = (jax.nn.sigmoid(mg[:, :D_MODEL].astype(F32)) * y_conv
             + jax.nn.sigmoid(mg[:, D_MODEL:].astype(F32)) * y_mlstm)
    out_ref[0] = x_ref[0] + jnp.dot(mixed.astype(BF16), wo_ref[...], preferred_element_type=F32)


def _merge(hf, hb, proj, u, x, head_g, w_conv_out, w_mlstm_out, w_out):
    B, S, _ = x.shape
    tm = min(MERGE_TM, S)
    row = lambda width, blk: pl.BlockSpec((1, tm, width), lambda b, i: (b, i, blk))
    return pl.pallas_call(
        _merge_kernel,
        grid=(B, S // tm),
        in_specs=[
            row(V_DIM, 0), row(V_DIM, 0),
            row(V_DIM, COL_O // V_DIM),
            row(2 * D_MODEL, COL_MERGE // (2 * D_MODEL)),
            row(CONV_DIM, 0),
            row(D_MODEL, 0),
            _resident((1, V_DIM)),
            _resident((CONV_DIM, D_MODEL)),
            _resident((V_DIM, D_MODEL)),
            _resident((D_MODEL, D_MODEL)),
        ],
        out_specs=row(D_MODEL, 0),
        out_shape=jax.ShapeDtypeStruct((B, S, D_MODEL), F32),
        compiler_params=_params("parallel", "parallel"),
        name="merge",
    )(hf, hb, proj, proj, u, x, head_g, w_conv_out, w_mlstm_out, w_out)


def _ffn_kernel(x_ref, prev_ref, next_ref, g_ref, wg_ref, wv_ref, cw_ref, cb_ref, wd_ref, gf_ref,
                out_ref, xn_ref, *, tm, final_norm):
    i, j = pl.program_id(1), pl.program_id(2)
    halo = FFN_HALO

    @pl.when(j == 0)
    def _():
        g = g_ref[...]
        xn_ref[0:tm, :] = _rms(x_ref[0], g).astype(BF16)
        edge = jnp.concatenate([
            jnp.where(i > 0, _rms(prev_ref[0], g), 0.0),
            jnp.where(i < pl.num_programs(1) - 1, _rms(next_ref[0], g), 0.0)], 0)
        xn_ref[tm:tm + 2 * halo, :] = edge.astype(BF16)
        out_ref[0] = x_ref[0]

    gate_all = jnp.dot(xn_ref[...], wg_ref[...], preferred_element_type=F32)
    val = jnp.dot(xn_ref[0:tm, :], wv_ref[...], preferred_element_type=F32)
    gate = gate_all[0:tm]
    before = gate_all[tm + halo - 1:tm + halo]
    after = gate_all[tm + halo:tm + halo + 1]
    row_id = lax.broadcasted_iota(jnp.int32, (SUBLANES, gate.shape[1]), 0)
    down = pltpu.roll(gate, 1, 0)
    down = jnp.concatenate([jnp.where(row_id == 0, before, down[0:SUBLANES]), down[SUBLANES:]], 0)
    up = pltpu.roll(gate, tm - 1, 0)
    up = jnp.concatenate([up[:tm - SUBLANES], jnp.where(row_id == SUBLANES - 1, after, up[tm - SUBLANES:])], 0)
    conv = cw_ref[0:1, :] * down + cw_ref[1:2, :] * gate + cw_ref[2:3, :] * up + cb_ref[...]
    hid = 0.5 * conv * (1.0 + lax.erf(conv * (2.0 ** -0.5))) * val
    wd_row = lax.broadcasted_iota(jnp.int32, wd_ref.shape, 0)
    wd = jnp.where(wd_row < D_FF - j * FFN_TF, wd_ref[...], jnp.zeros((), BF16))
    out_ref[0] += jnp.dot(hid.astype(BF16), wd, preferred_element_type=F32)

    if final_norm:
        @pl.when(j == pl.num_programs(2) - 1)
        def _():
            out_ref[0] = _rms(out_ref[0], gf_ref[...])


def _ffn(x, g, w_up, conv_w, conv_b, w_down, layer, g_final, final_norm):
    B, S, _ = x.shape
    tm = min(FFN_TM, S)
    halo = FFN_HALO
    per = tm // halo
    last = S // halo - 1
    return pl.pallas_call(
        functools.partial(_ffn_kernel, tm=tm, final_norm=final_norm),
        grid=(B, S // tm, D_FF_PAD // FFN_TF),
        in_specs=[
            pl.BlockSpec((1, tm, D_MODEL), lambda b, i, j: (b, i, 0), pipeline_mode=pl.Buffered(1)),
            pl.BlockSpec((1, halo, D_MODEL), lambda b, i, j: (b, jnp.maximum(i * per - 1, 0), 0)),
            pl.BlockSpec((1, halo, D_MODEL), lambda b, i, j: (b, jnp.minimum((i + 1) * per, last), 0)),
            pl.BlockSpec((1, D_MODEL), lambda b, i, j: (0, 0)),
            pl.BlockSpec((None, D_MODEL, FFN_TF), lambda b, i, j: (layer, 0, j)),
            pl.BlockSpec((None, D_MODEL, FFN_TF), lambda b, i, j: (layer, 0, D_FF_PAD // FFN_TF + j)),
            pl.BlockSpec((SUBLANES, FFN_TF), lambda b, i, j: (0, j)),
            pl.BlockSpec((1, FFN_TF), lambda b, i, j: (0, j)),
            pl.BlockSpec((None, FFN_TF, D_MODEL), lambda b, i, j: (layer, j, 0)),
            pl.BlockSpec((1, D_MODEL), lambda b, i, j: (0, 0)),
        ],
        out_specs=pl.BlockSpec((1, tm, D_MODEL), lambda b, i, j: (b, i, 0)),
        out_shape=jax.ShapeDtypeStruct((B, S, D_MODEL), F32),
        scratch_shapes=[pltpu.VMEM((tm + 2 * halo, D_MODEL), BF16)],
        compiler_params=_params("parallel", "parallel", "arbitrary"),
        name="ffn",
    )(x, x, x, g, w_up, w_up, conv_w, conv_b, w_down, g_final)


def _pad_to(a, axis, size):
    pad = [(0, 0)] * a.ndim
    pad[axis] = (0, size - a.shape[axis])
    return jnp.pad(a, pad)


def _cell_gates(a):
    g = a.reshape(a.shape[:-1] + (4, HEADS))
    ig = g[..., 0::2, :].reshape(a.shape[:-1] + (CELLS,))
    lf = g[..., 1::2, :].reshape(a.shape[:-1] + (CELLS,))
    return jnp.concatenate([_pad_to(ig, -1 % a.ndim, LANES), _pad_to(lf, -1 % a.ndim, LANES)], -1)


def _stacked_weights(w_in, w_up, w_down):
    wt = jnp.swapaxes(w_in, 1, 2)
    ff_pad = jnp.zeros(w_up.shape[:2] + (D_FF_PAD - D_FF,), w_up.dtype)
    return dict(
        w_main=jnp.concatenate([wt[:, :GATE_OFF], wt[:, GATE_OFF + N_GATES:]], 1).astype(BF16),
        w_up=jnp.concatenate([w_up[:, :, :D_FF], ff_pad, w_up[:, :, D_FF:], ff_pad], 2).astype(BF16),
        w_down=w_down.astype(BF16),
    )


def _layer_params(l, norm_mix_g, w_in, b_gates, conv_dw_w, conv_dw_b, conv_ln_g, conv_ln_b, w_conv_out,
                  mlstm_head_g, w_mlstm_out, w_out, norm_ffn_g, w_up, ffn_dw_w, ffn_dw_b, w_down):
    wt = jnp.swapaxes(w_in[l], 0, 1)
    row = lambda a: a[l].reshape(1, -1)
    return dict(
        norm_mix_g=row(norm_mix_g),
        w_gate=_cell_gates(wt[GATE_OFF:GATE_OFF + N_GATES].T).astype(BF16),
        b_gates=_cell_gates(row(b_gates)),
        conv_w=jnp.broadcast_to(conv_dw_w[l][:, None, :], (CONV_WIDTH, SUBLANES, CONV_DIM)),
        conv_b=row(conv_dw_b), ln_g=row(conv_ln_g), ln_b=row(conv_ln_b),
        w_conv_out=w_conv_out[l].astype(BF16),
        head_g=row(mlstm_head_g),
        w_mlstm_out=w_mlstm_out[l].astype(BF16),
        w_out=w_out[l].astype(BF16),
        norm_ffn_g=row(norm_ffn_g),
        ffn_w=_pad_to(_pad_to(ffn_dw_w[l], 0, SUBLANES), 1, D_FF_PAD),
        ffn_b=_pad_to(row(ffn_dw_b), 1, D_FF_PAD),
    )


def _trunk(x, layers, big, g_final):
    for l, p in enumerate(layers):
        proj, gates = _in_proj(x, p["norm_mix_g"], big["w_main"], l, p["w_gate"])
        tab, row_c, dec = _gate_scan(gates, p["b_gates"])
        u = _conv_branch(proj, p["conv_w"], p["conv_b"], p["ln_g"], p["ln_b"])
        hf, hb = _mlstm(proj, tab, row_c, dec)
        x = _merge(hf, hb, proj, u, x, p["head_g"], p["w_conv_out"], p["w_mlstm_out"], p["w_out"])
        x = _ffn(x, p["norm_ffn_g"], big["w_up"], p["ffn_w"], p["ffn_b"], big["w_down"], l,
                 g_final, final_norm=(l == len(layers) - 1))
    return x


def kernel(x_prompt, x_sample, norm_mix_g, w_in, b_gates, conv_dw_w, conv_dw_b, conv_ln_g, conv_ln_b,
           w_conv_out, mlstm_head_g, w_mlstm_out, w_out, norm_ffn_g, w_up, ffn_dw_w, ffn_dw_b, w_down,
           norm_final_g):
    depth = w_in.shape[0]
    layers = [
        _layer_params(l, norm_mix_g, w_in, b_gates, conv_dw_w, conv_dw_b, conv_ln_g, conv_ln_b, w_conv_out,
                      mlstm_head_g, w_mlstm_out, w_out, norm_ffn_g, w_up, ffn_dw_w, ffn_dw_b, w_down)
        for l in range(depth)
    ]
    big = _stacked_weights(w_in, w_up, w_down)
    g_final = norm_final_g.reshape(1, -1)
    return (_trunk(x_prompt, layers, big, g_final), _trunk(x_sample, layers, big, g_final))
```

```python
import functools

import jax
import jax.numpy as jnp
from jax import lax
from jax.experimental import pallas as pl
from jax.experimental.pallas import tpu as pltpu

F32 = jnp.float32
BF16 = jnp.bfloat16

D_MODEL = 2048
CONV_DIM = 1024
CONV_WIDTH = 31
CONV_PAD = (CONV_WIDTH - 1) // 2
HEADS = 4
CELLS = 2 * HEADS
DK = 256
DV = 512
QK_DIM = HEADS * DK
V_DIM = HEADS * DV
CHUNK = 256
D_FF = 5504
FFN_CONV_WIDTH = 3
EPS = 1e-6

LANES = 128
SUBLANES = 8
BF16_SUBLANES = 16
VMEM_LIMIT = 56 * 1024 * 1024

COL_CONV = 0
COL_Q = 2 * CONV_DIM
COL_K = COL_Q + QK_DIM
COL_V = COL_K + QK_DIM
COL_O = COL_V + V_DIM
COL_MERGE = COL_O + V_DIM
D_MAIN = COL_MERGE + 2 * D_MODEL
N_GATES = 4 * HEADS
GATE_OFF = COL_O + V_DIM
GATE_W = 2 * LANES

G_ALPHA, G_INTER, G_FLOOR, G_KSCALE, G_GROUPS = 0, 1, 2, 3, 4

IN_TM, IN_TN = 1024, 2048
SCAN_CHUNKS = 4
CONV_TM, CONV_ROWS = 512, 32
CONV_HALO = BF16_SUBLANES
MERGE_TM = 256
FFN_TM, FFN_TF = 1024, 512
D_FF_PAD = -(-D_FF // FFN_TF) * FFN_TF
FFN_HALO = SUBLANES


def _params(*sem):
    return pltpu.CompilerParams(dimension_semantics=sem, vmem_limit_bytes=VMEM_LIMIT)


def _resident(shape):
    return pl.BlockSpec(shape, lambda *_: (0,) * len(shape), pipeline_mode=pl.Buffered(1))


def _rms(x, g):
    return x * lax.rsqrt(jnp.mean(x * x, -1, keepdims=True) + EPS) * g


def _in_proj_kernel(x_ref, g_ref, w_ref, wg_ref, proj_ref, gates_ref, xn_ref):
    @pl.when(pl.program_id(2) == 0)
    def _():
        xn = _rms(x_ref[0], g_ref[...]).astype(BF16)
        xn_ref[...] = xn
        gates_ref[0] = jnp.dot(xn, wg_ref[...], preferred_element_type=F32)

    proj_ref[0] = lax.dot_general(xn_ref[...], w_ref[...], (((1,), (1,)), ((), ())),
                                  preferred_element_type=F32).astype(BF16)


def _in_proj(x, g, w_main, layer, w_gate):
    B, S, _ = x.shape
    tm = min(IN_TM, S)
    return pl.pallas_call(
        _in_proj_kernel,
        grid=(B, S // tm, D_MAIN // IN_TN),
        in_specs=[
            pl.BlockSpec((1, tm, D_MODEL), lambda b, i, j: (b, i, 0)),
            pl.BlockSpec((1, D_MODEL), lambda b, i, j: (0, 0)),
            pl.BlockSpec((None, IN_TN, D_MODEL), lambda b, i, j: (layer, j, 0)),
            pl.BlockSpec((D_MODEL, GATE_W), lambda b, i, j: (0, 0)),
        ],
        out_specs=[
            pl.BlockSpec((1, tm, IN_TN), lambda b, i, j: (b, i, j)),
            pl.BlockSpec((1, tm, GATE_W), lambda b, i, j: (b, i, 0)),
        ],
        out_shape=[
            jax.ShapeDtypeStruct((B, S, D_MAIN), BF16),
            jax.ShapeDtypeStruct((B, S, GATE_W), F32),
        ],
        scratch_shapes=[pltpu.VMEM((tm, D_MODEL), BF16)],
        compiler_params=_params("parallel", "parallel", "arbitrary"),
        name="in_proj",
    )(x, g, w_main, w_gate)


def _log_sigmoid(x):
    return jnp.minimum(x, 0.0) - jnp.log1p(jnp.exp(-jnp.abs(x)))


def _split3(x):
    hi = x.astype(BF16)
    r = x - hi.astype(F32)
    mid = r.astype(BF16)
    lo = (r - mid.astype(F32)).astype(BF16)
    return hi, mid, lo


def _gate_scan_kernel(rawf_ref, rawb_ref, bias_ref, tab_ref, row_ref, dec_ref, m_ref):
    @pl.when(pl.program_id(1) == 0)
    def _():
        m_ref[...] = jnp.zeros_like(m_ref)

    n = CHUNK
    lane = lax.broadcasted_iota(jnp.int32, (n, LANES), 1)
    row_id = lax.broadcasted_iota(jnp.int32, (n, LANES), 0)
    is_f = lane < HEADS
    r = lax.broadcasted_iota(jnp.int32, (n, n), 0)
    c = lax.broadcasted_iota(jnp.int32, (n, n), 1)
    tril = (c <= r).astype(BF16)
    triu = (c >= r).astype(BF16)
    m = m_ref[...]
    for i in range(SCAN_CHUNKS):
        lo, hi = i * n, (i + 1) * n
        rf = rawf_ref[0, lo:hi, :]
        rb = rawb_ref[0, (SCAN_CHUNKS - 1 - i) * n:(SCAN_CHUNKS - i) * n, :]
        ig = jnp.where(is_f, rf[:, :LANES], rb[:, :LANES]) + bias_ref[:, :LANES]
        lf = _log_sigmoid(jnp.where(is_f, rf[:, LANES:], rb[:, LANES:]) + bias_ref[:, LANES:])
        cs_f = jnp.zeros_like(lf)
        cs_b = jnp.zeros_like(lf)
        for part in _split3(lf):
            cs_f = cs_f + jnp.dot(tril, part, preferred_element_type=F32)
            cs_b = cs_b + jnp.dot(triu, part, preferred_element_type=F32)
        b = jnp.where(is_f, cs_f, cs_b)
        cc = ig - b
        cm = cc
        k = 1
        while k < n:
            dn = jnp.where(row_id >= k, pltpu.roll(cm, k, 0), -jnp.inf)
            up = jnp.where(row_id < n - k, pltpu.roll(cm, n - k, 0), -jnp.inf)
            cm = jnp.maximum(cm, jnp.where(is_f, dn, up))
            k *= 2
        mm = jnp.maximum(m, cm)
        total = jnp.where(is_f[:1], b[n - 1:n], b[0:1])
        m_end = jnp.where(is_f[:1], mm[n - 1:n], mm[0:1])
        tab_ref[0, lo:hi, G_ALPHA * LANES:(G_ALPHA + 1) * LANES] = -mm
        tab_ref[0, lo:hi, G_INTER * LANES:(G_INTER + 1) * LANES] = jnp.exp(m - mm)
        tab_ref[0, lo:hi, G_FLOOR * LANES:(G_FLOOR + 1) * LANES] = jnp.exp(-(b + mm))
        tab_ref[0, lo:hi, G_KSCALE * LANES:(G_KSCALE + 1) * LANES] = jnp.exp(cc - m_end)
        row_ref[0, :, lo:hi] = cc.T[:CELLS, :]
        dec_ref[0, i] = jnp.exp(m - m_end)
        m = total + m_end
    m_ref[...] = m


def _gate_scan(raw, bias):
    B, S, _ = raw.shape
    rows = SCAN_CHUNKS * CHUNK
    nj = S // rows
    return pl.pallas_call(
        _gate_scan_kernel,
        grid=(B, nj),
        in_specs=[
            pl.BlockSpec((1, rows, GATE_W), lambda b, j: (b, j, 0)),
            pl.BlockSpec((1, rows, GATE_W), lambda b, j: (b, nj - 1 - j, 0)),
            pl.BlockSpec((1, GATE_W), lambda b, j: (0, 0)),
        ],
        out_specs=[
            pl.BlockSpec((1, rows, G_GROUPS * LANES), lambda b, j: (b, j, 0)),
            pl.BlockSpec((1, CELLS, rows), lambda b, j: (b, 0, j)),
            pl.BlockSpec((1, SCAN_CHUNKS, 1, LANES), lambda b, j: (b, j, 0, 0)),
        ],
        out_shape=[
            jax.ShapeDtypeStruct((B, S, G_GROUPS * LANES), F32),
            jax.ShapeDtypeStruct((B, CELLS, S), F32),
            jax.ShapeDtypeStruct((B, S // CHUNK, 1, LANES), F32),
        ],
        scratch_shapes=[pltpu.VMEM((1, LANES), F32)],
        compiler_params=_params("parallel", "arbitrary"),
        name="gate_scan",
    )(raw, raw, bias)


def _conv_kernel(main_ref, prev_ref, next_ref, w_ref, b_ref, lg_ref, lb_ref, out_ref, u_ref, *, tm):
    i = pl.program_id(1)
    halo = CONV_HALO
    span = tm + 2 * halo - SUBLANES

    def glu(blk):
        return blk[:, :CONV_DIM].astype(F32) * jax.nn.sigmoid(blk[:, CONV_DIM:].astype(F32))

    u_ref[0, 0:halo, :] = jnp.where(i > 0, glu(prev_ref[0]), 0.0)
    u_ref[0, halo:halo + tm, :] = glu(main_ref[0])
    u_ref[0, halo + tm:2 * halo + tm, :] = jnp.where(i < pl.num_programs(1) - 1, glu(next_ref[0]), 0.0)
    for s in range(1, SUBLANES):
        u_ref[s, 0:span, :] = u_ref[0, s:s + span, :]

    grp = CONV_ROWS // SUBLANES
    for r0 in range(0, tm, CONV_ROWS):
        acc = jnp.broadcast_to(b_ref[...].reshape(1, 1, CONV_DIM), (grp, SUBLANES, CONV_DIM))
        for j in range(CONV_WIDTH):
            off = r0 + halo - CONV_PAD + j
            lo = off - off % SUBLANES
            blk = u_ref[off % SUBLANES, lo:lo + CONV_ROWS, :].reshape(grp, SUBLANES, CONV_DIM)
            acc = acc + w_ref[j][None] * blk
        acc = acc.reshape(CONV_ROWS, CONV_DIM)
        mu = jnp.mean(acc, -1, keepdims=True)
        xc = acc - mu
        y = xc * lax.rsqrt(jnp.mean(xc * xc, -1, keepdims=True) + EPS) * lg_ref[...] + lb_ref[...]
        out_ref[0, r0:r0 + CONV_ROWS, :] = (y * jax.nn.sigmoid(y)).astype(BF16)


def _conv_branch(proj, w, b, lg, lb):
    B, S, _ = proj.shape
    tm = min(CONV_TM, S)
    halo = CONV_HALO
    per = tm // halo
    last = S // halo - 1
    return pl.pallas_call(
        functools.partial(_conv_kernel, tm=tm),
        grid=(B, S // tm),
        in_specs=[
            pl.BlockSpec((1, tm, 2 * CONV_DIM), lambda b, i: (b, i, 0)),
            pl.BlockSpec((1, halo, 2 * CONV_DIM), lambda b, i: (b, jnp.maximum(i * per - 1, 0), 0)),
            pl.BlockSpec((1, halo, 2 * CONV_DIM), lambda b, i: (b, jnp.minimum((i + 1) * per, last), 0)),
            pl.BlockSpec((CONV_WIDTH, SUBLANES, CONV_DIM), lambda b, i: (0, 0, 0)),
            pl.BlockSpec((1, CONV_DIM), lambda b, i: (0, 0)),
            pl.BlockSpec((1, CONV_DIM), lambda b, i: (0, 0)),
            pl.BlockSpec((1, CONV_DIM), lambda b, i: (0, 0)),
        ],
        out_specs=pl.BlockSpec((1, tm, CONV_DIM), lambda b, i: (b, i, 0)),
        out_shape=jax.ShapeDtypeStruct((B, S, CONV_DIM), BF16),
        scratch_shapes=[pltpu.VMEM((SUBLANES, tm + 2 * halo, CONV_DIM), F32)],
        compiler_params=_params("parallel", "parallel"),
        name="conv_branch",
    )(proj, proj, proj, w, b, lg, lb)


def _mlstm_cell(q, k, v, alpha, c_row, inter, floor, kscale, decay, mask, c_ref, n_ref, idx):
    scale = DK ** -0.5
    c_state = c_ref[idx]
    n_state = n_ref[idx]
    w = jnp.exp(jnp.where(mask, alpha + c_row, -jnp.inf))
    qk = lax.dot_general(q, k, (((1,), (1,)), ((), ())), preferred_element_type=F32)
    s = qk * scale * w
    q_c = jnp.dot(q, c_state.astype(BF16), preferred_element_type=F32) * scale
    wk = kscale * k.astype(F32)
    both = jnp.dot(jnp.concatenate([s.astype(BF16), wk.astype(BF16).T], 0), v, preferred_element_type=F32)
    num = inter * q_c + both[:CHUNK]
    q_n = jnp.sum(q.astype(F32) * n_state, -1, keepdims=True) * scale
    den = inter * q_n + jnp.sum(s, -1, keepdims=True)
    h = num * (1.0 / jnp.maximum(jnp.abs(den), floor))
    c_ref[idx] = decay * c_state + both[CHUNK:]
    n_ref[idx] = decay * n_state + jnp.sum(wk, 0, keepdims=True)
    return h


def _mlstm_kernel(qf_ref, kf_ref, vf_ref, qb_ref, kb_ref, vb_ref, tab_ref, row_ref, dec_ref,
                  hf_ref, hb_ref, c_ref, n_ref):
    @pl.when(pl.program_id(1) == 0)
    def _():
        c_ref[...] = jnp.zeros_like(c_ref)
        n_ref[...] = jnp.zeros_like(n_ref)

    r = lax.broadcasted_iota(jnp.int32, (CHUNK, CHUNK), 0)
    c = lax.broadcasted_iota(jnp.int32, (CHUNK, CHUNK), 1)
    dirs = ((qf_ref, kf_ref, vf_ref, hf_ref, c <= r), (qb_ref, kb_ref, vb_ref, hb_ref, c >= r))
    decay = dec_ref[0, 0]
    for d, (q_ref, k_ref, v_ref, h_ref, mask) in enumerate(dirs):
        for h in range(HEADS):
            cell = d * HEADS + h
            col = lambda grp: tab_ref[0, :, grp * LANES + cell:grp * LANES + cell + 1]
            out = _mlstm_cell(
                q_ref[0, :, h * DK:(h + 1) * DK], k_ref[0, :, h * DK:(h + 1) * DK],
                v_ref[0, :, h * DV:(h + 1) * DV],
                col(G_ALPHA), row_ref[0, cell:cell + 1, :], col(G_INTER), col(G_FLOOR), col(G_KSCALE),
                decay[:, cell:cell + 1], mask, c_ref, n_ref, cell)
            h_ref[0, :, h * DV:(h + 1) * DV] = out.astype(BF16)


def _mlstm(proj, tab, row_c, dec):
    B, S, _ = proj.shape
    nc = S // CHUNK
    fwd = lambda blk: (lambda b, c: (b, c, blk))
    bwd = lambda blk: (lambda b, c: (b, nc - 1 - c, blk))
    qkv = lambda im: [
        pl.BlockSpec((1, CHUNK, QK_DIM), im(COL_Q // QK_DIM)),
        pl.BlockSpec((1, CHUNK, QK_DIM), im(COL_K // QK_DIM)),
        pl.BlockSpec((1, CHUNK, V_DIM), im(COL_V // V_DIM)),
    ]
    return pl.pallas_call(
        _mlstm_kernel,
        grid=(B, nc),
        in_specs=qkv(fwd) + qkv(bwd) + [
            pl.BlockSpec((1, CHUNK, G_GROUPS * LANES), fwd(0)),
            pl.BlockSpec((1, CELLS, CHUNK), lambda b, c: (b, 0, c)),
            pl.BlockSpec((1, 1, 1, LANES), lambda b, c: (b, c, 0, 0)),
        ],
        out_specs=[
            pl.BlockSpec((1, CHUNK, V_DIM), fwd(0)),
            pl.BlockSpec((1, CHUNK, V_DIM), bwd(0)),
        ],
        out_shape=[jax.ShapeDtypeStruct((B, S, V_DIM), BF16)] * 2,
        scratch_shapes=[
            pltpu.VMEM((CELLS, DK, DV), F32),
            pltpu.VMEM((CELLS, 1, DK), F32),
        ],
        compiler_params=_params("parallel", "arbitrary"),
        name="mlstm",
    )(proj, proj, proj, proj, proj, proj, tab, row_c, dec)


def _merge_kernel(hf_ref, hb_ref, o_ref, mg_ref, u_ref, x_ref, hg_ref, wc_ref, wm_ref, wo_ref, out_ref):
    hs = hf_ref[0].astype(F32) + hb_ref[0].astype(F32)
    heads = []
    for h in range(HEADS):
        blk = hs[:, h * DV:(h + 1) * DV]
        heads.append(_rms(blk, hg_ref[:, h * DV:(h + 1) * DV]))
    hn = jnp.concatenate(heads, -1) * jax.nn.sigmoid(o_ref[0].astype(F32))
    y_mlstm = jnp.dot(hn.astype(BF16), wm_ref[...], preferred_element_type=F32)
    y_conv = jnp.dot(u_ref[0], wc_ref[...], preferred_element_type=F32)
    mg = mg_ref[0]
    mixed = (jax.nn.sigmoid(mg[:, :D_MODEL].astype(F32)) * y_conv
             + jax.nn.sigmoid(mg[:, D_MODEL:].astype(F32)) * y_mlstm)
    out_ref[0] = x_ref[0] + jnp.dot(mixed.astype(BF16), wo_ref[...], preferred_element_type=F32)


def _merge(hf, hb, proj, u, x, head_g, w_conv_out, w_mlstm_out, w_out):
    B, S, _ = x.shape
    tm = min(MERGE_TM, S)
    row = lambda width, blk: pl.BlockSpec((1, tm, width), lambda b, i: (b, i, blk))
    return pl.pallas_call(
        _merge_kernel,
        grid=(B, S // tm),
        in_specs=[
            row(V_DIM, 0), row(V_DIM, 0),
            row(V_DIM, COL_O // V_DIM),
            row(2 * D_MODEL, COL_MERGE // (2 * D_MODEL)),
            row(CONV_DIM, 0),
            row(D_MODEL, 0),
            _resident((1, V_DIM)),
            _resident((CONV_DIM, D_MODEL)),
            _resident((V_DIM, D_MODEL)),
            _resident((D_MODEL, D_MODEL)),
        ],
        out_specs=row(D_MODEL, 0),
        out_shape=jax.ShapeDtypeStruct((B, S, D_MODEL), F32),
        compiler_params=_params("parallel", "parallel"),
        name="merge",
    )(hf, hb, proj, proj, u, x, head_g, w_conv_out, w_mlstm_out, w_out)


def _ffn_kernel(x_ref, prev_ref, next_ref, g_ref, wg_ref, wv_ref, cw_ref, cb_ref, wd_ref, gf_ref,
                out_ref, xn_ref, *, tm, final_norm):
    i, j = pl.program_id(1), pl.program_id(2)
    halo = FFN_HALO

    @pl.when(j == 0)
    def _():
        g = g_ref[...]
        xn_ref[0:tm, :] = _rms(x_ref[0], g).astype(BF16)
        edge = jnp.concatenate([
            jnp.where(i > 0, _rms(prev_ref[0], g), 0.0),
            jnp.where(i < pl.num_programs(1) - 1, _rms(next_ref[0], g), 0.0)], 0)
        xn_ref[tm:tm + 2 * halo, :] = edge.astype(BF16)
        out_ref[0] = x_ref[0]

    valid = D_FF - j * FFN_TF
    up_ok = lax.broadcasted_iota(jnp.int32, wg_ref.shape, 1) < valid
    zero = jnp.zeros((), BF16)
    gate_all = jnp.dot(xn_ref[...], jnp.where(up_ok, wg_ref[...], zero), preferred_element_type=F32)
    val = jnp.dot(xn_ref[0:tm, :], jnp.where(up_ok, wv_ref[...], zero), preferred_element_type=F32)
    gate = gate_all[0:tm]
    before = gate_all[tm + halo - 1:tm + halo]
    after = gate_all[tm + halo:tm + halo + 1]
    row_id = lax.broadcasted_iota(jnp.int32, (SUBLANES, gate.shape[1]), 0)
    down = pltpu.roll(gate, 1, 0)
    down = jnp.concatenate([jnp.where(row_id == 0, before, down[0:SUBLANES]), down[SUBLANES:]], 0)
    up = pltpu.roll(gate, tm - 1, 0)
    up = jnp.concatenate([up[:tm - SUBLANES], jnp.where(row_id == SUBLANES - 1, after, up[tm - SUBLANES:])], 0)
    conv = cw_ref[0:1, :] * down + cw_ref[1:2, :] * gate + cw_ref[2:3, :] * up + cb_ref[...]
    hid = 0.5 * conv * (1.0 + lax.erf(conv * (2.0 ** -0.5))) * val
    wd = jnp.where(lax.broadcasted_iota(jnp.int32, wd_ref.shape, 0) < valid, wd_ref[...], zero)
    out_ref[0] += jnp.dot(hid.astype(BF16), wd, preferred_element_type=F32)

    if final_norm:
        @pl.when(j == pl.num_programs(2) - 1)
        def _():
            out_ref[0] = _rms(out_ref[0], gf_ref[...])


def _ffn(x, g, w_gate, w_val, conv_w, conv_b, w_down, layer, g_final, final_norm):
    B, S, _ = x.shape
    tm = min(FFN_TM, S)
    halo = FFN_HALO
    per = tm // halo
    last = S // halo - 1
    return pl.pallas_call(
        functools.partial(_ffn_kernel, tm=tm, final_norm=final_norm),
        grid=(B, S // tm, D_FF_PAD // FFN_TF),
        in_specs=[
            pl.BlockSpec((1, tm, D_MODEL), lambda b, i, j: (b, i, 0), pipeline_mode=pl.Buffered(1)),
            pl.BlockSpec((1, halo, D_MODEL), lambda b, i, j: (b, jnp.maximum(i * per - 1, 0), 0)),
            pl.BlockSpec((1, halo, D_MODEL), lambda b, i, j: (b, jnp.minimum((i + 1) * per, last), 0)),
            pl.BlockSpec((1, D_MODEL), lambda b, i, j: (0, 0)),
            pl.BlockSpec((None, D_MODEL, FFN_TF), lambda b, i, j: (layer, 0, j)),
            pl.BlockSpec((None, D_MODEL, FFN_TF), lambda b, i, j: (layer, 0, j)),
            pl.BlockSpec((SUBLANES, FFN_TF), lambda b, i, j: (0, j)),
            pl.BlockSpec((1, FFN_TF), lambda b, i, j: (0, j)),
            pl.BlockSpec((None, FFN_TF, D_MODEL), lambda b, i, j: (layer, j, 0)),
            pl.BlockSpec((1, D_MODEL), lambda b, i, j: (0, 0)),
        ],
        out_specs=pl.BlockSpec((1, tm, D_MODEL), lambda b, i, j: (b, i, 0)),
        out_shape=jax.ShapeDtypeStruct((B, S, D_MODEL), F32),
        scratch_shapes=[pltpu.VMEM((tm + 2 * halo, D_MODEL), BF16)],
        compiler_params=_params("parallel", "parallel", "arbitrary"),
        name="ffn",
    )(x, x, x, g, w_gate, w_val, conv_w, conv_b, w_down, g_final)


def _pad_to(a, axis, size):
    pad = [(0, 0)] * a.ndim
    pad[axis] = (0, size - a.shape[axis])
    return jnp.pad(a, pad)


def _cell_gates(a):
    g = a.reshape(a.shape[:-1] + (4, HEADS))
    ig = g[..., 0::2, :].reshape(a.shape[:-1] + (CELLS,))
    lf = g[..., 1::2, :].reshape(a.shape[:-1] + (CELLS,))
    return jnp.concatenate([_pad_to(ig, -1 % a.ndim, LANES), _pad_to(lf, -1 % a.ndim, LANES)], -1)


def _stacked_weights(w_in, w_up, w_down):
    wt = jnp.swapaxes(w_in, 1, 2)
    return dict(
        w_main=jnp.where(lax.broadcasted_iota(jnp.int32, (1, D_MAIN, 1), 1) < GATE_OFF,
                         wt[:, :D_MAIN], wt[:, N_GATES:N_GATES + D_MAIN]).astype(BF16),
        w_gates=wt[:, GATE_OFF:GATE_OFF + N_GATES].astype(BF16),
        w_up_gate=w_up[:, :, :D_FF].astype(BF16),
        w_up_val=w_up[:, :, D_FF:].astype(BF16),
        w_down=w_down.astype(BF16),
    )


def _layer_params(l, norm_mix_g, w_gates, b_gates, conv_dw_w, conv_dw_b, conv_ln_g, conv_ln_b, w_conv_out,
                  mlstm_head_g, w_mlstm_out, w_out, norm_ffn_g, w_up, ffn_dw_w, ffn_dw_b, w_down):
    row = lambda a: a[l].reshape(1, -1)
    return dict(
        norm_mix_g=row(norm_mix_g),
        w_gate=_cell_gates(w_gates[l].T),
        b_gates=_cell_gates(row(b_gates)),
        conv_w=jnp.broadcast_to(conv_dw_w[l][:, None, :], (CONV_WIDTH, SUBLANES, CONV_DIM)),
        conv_b=row(conv_dw_b), ln_g=row(conv_ln_g), ln_b=row(conv_ln_b),
        w_conv_out=w_conv_out[l].astype(BF16),
        head_g=row(mlstm_head_g),
        w_mlstm_out=w_mlstm_out[l].astype(BF16),
        w_out=w_out[l].astype(BF16),
        norm_ffn_g=row(norm_ffn_g),
        ffn_w=_pad_to(_pad_to(ffn_dw_w[l], 0, SUBLANES), 1, D_FF_PAD),
        ffn_b=_pad_to(row(ffn_dw_b), 1, D_FF_PAD),
    )


def _trunk(x, layers, big, g_final):
    for l, p in enumerate(layers):
        proj, gates = _in_proj(x, p["norm_mix_g"], big["w_main"], l, p["w_gate"])
        tab, row_c, dec = _gate_scan(gates, p["b_gates"])
        u = _conv_branch(proj, p["conv_w"], p["conv_b"], p["ln_g"], p["ln_b"])
        hf, hb = _mlstm(proj, tab, row_c, dec)
        x = _merge(hf, hb, proj, u, x, p["head_g"], p["w_conv_out"], p["w_mlstm_out"], p["w_out"])
        x = _ffn(x, p["norm_ffn_g"], big["w_up_gate"], big["w_up_val"], p["ffn_w"], p["ffn_b"], big["w_down"], l,
                 g_final, final_norm=(l == len(layers) - 1))
    return x


def kernel(x_prompt, x_sample, norm_mix_g, w_in, b_gates, conv_dw_w, conv_dw_b, conv_ln_g, conv_ln_b,
           w_conv_out, mlstm_head_g, w_mlstm_out, w_out, norm_ffn_g, w_up, ffn_dw_w, ffn_dw_b, w_down,
           norm_final_g):
    depth = w_in.shape[0]
    big = _stacked_weights(w_in, w_up, w_down)
    layers = [
        _layer_params(l, norm_mix_g, big["w_gates"], b_gates, conv_dw_w, conv_dw_b, conv_ln_g, conv_ln_b, w_conv_out,
                      mlstm_head_g, w_mlstm_out, w_out, norm_ffn_g, w_up, ffn_dw_w, ffn_dw_b, w_down)
        for l in range(depth)
    ]
    g_final = norm_final_g.reshape(1, -1)
    return (_trunk(x_prompt, layers, big, g_final), _trunk(x_sample, layers, big, g_final))
```

```python
import functools

import jax
import jax.numpy as jnp
from jax import lax
from jax.experimental import pallas as pl
from jax.experimental.pallas import tpu as pltpu

F32 = jnp.float32
BF16 = jnp.bfloat16

D_MODEL = 2048
CONV_DIM = 1024
CONV_WIDTH = 31
CONV_PAD = (CONV_WIDTH - 1) // 2
HEADS = 4
CELLS = 2 * HEADS
DK = 256
DV = 512
QK_DIM = HEADS * DK
V_DIM = HEADS * DV
CHUNK = 256
D_FF = 5504
FFN_CONV_WIDTH = 3
EPS = 1e-6

LANES = 128
SUBLANES = 8
BF16_SUBLANES = 16
VMEM_LIMIT = 56 * 1024 * 1024
FFN_VMEM_LIMIT = 61 * 1024 * 1024

COL_CONV = 0
COL_Q = 2 * CONV_DIM
COL_K = COL_Q + QK_DIM
COL_V = COL_K + QK_DIM
COL_O = COL_V + V_DIM
COL_MERGE = COL_O + V_DIM
D_MAIN = COL_MERGE + 2 * D_MODEL
N_GATES = 4 * HEADS
GATE_OFF = COL_O + V_DIM
GATE_W = 2 * LANES

G_ALPHA, G_INTER, G_FLOOR, G_KSCALE, G_GROUPS = 0, 1, 2, 3, 4

IN_TM, IN_TN = 1024, 2048
SCAN_CHUNKS = 4
CONV_TM, CONV_ROWS = 512, 32
CONV_HALO = BF16_SUBLANES
MERGE_TM = 256
FFN_TM, FFN_TF = 1024, 512
D_FF_PAD = -(-D_FF // FFN_TF) * FFN_TF
FFN_HALO = SUBLANES


def _params(*sem, vmem_limit=VMEM_LIMIT):
    return pltpu.CompilerParams(dimension_semantics=sem, vmem_limit_bytes=vmem_limit)


def _resident(shape):
    return pl.BlockSpec(shape, lambda *_: (0,) * len(shape), pipeline_mode=pl.Buffered(1))


def _rms(x, g):
    return x * lax.rsqrt(jnp.mean(x * x, -1, keepdims=True) + EPS) * g


def _in_proj_kernel(x_ref, g_ref, w_ref, wg_ref, proj_ref, gates_ref, xn_ref):
    @pl.when(pl.program_id(2) == 0)
    def _():
        xn = _rms(x_ref[0], g_ref[...]).astype(BF16)
        xn_ref[...] = xn
        gates_ref[0] = jnp.dot(xn, wg_ref[...], preferred_element_type=F32)

    proj_ref[0] = lax.dot_general(xn_ref[...], w_ref[...], (((1,), (1,)), ((), ())),
                                  preferred_element_type=F32).astype(BF16)


def _in_proj(x, g, w_main, layer, w_gate):
    B, S, _ = x.shape
    tm = min(IN_TM, S)
    return pl.pallas_call(
        _in_proj_kernel,
        grid=(B, S // tm, D_MAIN // IN_TN),
        in_specs=[
            pl.BlockSpec((1, tm, D_MODEL), lambda b, i, j: (b, i, 0)),
            pl.BlockSpec((1, D_MODEL), lambda b, i, j: (0, 0)),
            pl.BlockSpec((None, IN_TN, D_MODEL), lambda b, i, j: (layer, j, 0)),
            pl.BlockSpec((D_MODEL, GATE_W), lambda b, i, j: (0, 0)),
        ],
        out_specs=[
            pl.BlockSpec((1, tm, IN_TN), lambda b, i, j: (b, i, j)),
            pl.BlockSpec((1, tm, GATE_W), lambda b, i, j: (b, i, 0)),
        ],
        out_shape=[
            jax.ShapeDtypeStruct((B, S, D_MAIN), BF16),
            jax.ShapeDtypeStruct((B, S, GATE_W), F32),
        ],
        scratch_shapes=[pltpu.VMEM((tm, D_MODEL), BF16)],
        compiler_params=_params("parallel", "parallel", "arbitrary"),
        name="in_proj",
    )(x, g, w_main, w_gate)


def _log_sigmoid(x):
    return jnp.minimum(x, 0.0) - jnp.log1p(jnp.exp(-jnp.abs(x)))


def _split3(x):
    hi = x.astype(BF16)
    r = x - hi.astype(F32)
    mid = r.astype(BF16)
    lo = (r - mid.astype(F32)).astype(BF16)
    return hi, mid, lo


def _gate_scan_kernel(rawf_ref, rawb_ref, bias_ref, tab_ref, row_ref, dec_ref, m_ref):
    @pl.when(pl.program_id(1) == 0)
    def _():
        m_ref[...] = jnp.zeros_like(m_ref)

    n = CHUNK
    lane = lax.broadcasted_iota(jnp.int32, (n, LANES), 1)
    row_id = lax.broadcasted_iota(jnp.int32, (n, LANES), 0)
    is_f = lane < HEADS
    r = lax.broadcasted_iota(jnp.int32, (n, n), 0)
    c = lax.broadcasted_iota(jnp.int32, (n, n), 1)
    tril = (c <= r).astype(BF16)
    triu = (c >= r).astype(BF16)
    m = m_ref[...]
    for i in range(SCAN_CHUNKS):
        lo, hi = i * n, (i + 1) * n
        rf = rawf_ref[0, lo:hi, :]
        rb = rawb_ref[0, (SCAN_CHUNKS - 1 - i) * n:(SCAN_CHUNKS - i) * n, :]
        ig = jnp.where(is_f, rf[:, :LANES], rb[:, :LANES]) + bias_ref[:, :LANES]
        lf = _log_sigmoid(jnp.where(is_f, rf[:, LANES:], rb[:, LANES:]) + bias_ref[:, LANES:])
        cs_f = jnp.zeros_like(lf)
        cs_b = jnp.zeros_like(lf)
        for part in _split3(lf):
            cs_f = cs_f + jnp.dot(tril, part, preferred_element_type=F32)
            cs_b = cs_b + jnp.dot(triu, part, preferred_element_type=F32)
        b = jnp.where(is_f, cs_f, cs_b)
        cc = ig - b
        cm = cc
        k = 1
        while k < n:
            dn = jnp.where(row_id >= k, pltpu.roll(cm, k, 0), -jnp.inf)
            up = jnp.where(row_id < n - k, pltpu.roll(cm, n - k, 0), -jnp.inf)
            cm = jnp.maximum(cm, jnp.where(is_f, dn, up))
            k *= 2
        mm = jnp.maximum(m, cm)
        total = jnp.where(is_f[:1], b[n - 1:n], b[0:1])
        m_end = jnp.where(is_f[:1], mm[n - 1:n], mm[0:1])
        tab_ref[0, lo:hi, G_ALPHA * LANES:(G_ALPHA + 1) * LANES] = -mm
        tab_ref[0, lo:hi, G_INTER * LANES:(G_INTER + 1) * LANES] = jnp.exp(m - mm)
        tab_ref[0, lo:hi, G_FLOOR * LANES:(G_FLOOR + 1) * LANES] = jnp.exp(-(b + mm))
        tab_ref[0, lo:hi, G_KSCALE * LANES:(G_KSCALE + 1) * LANES] = jnp.exp(cc - m_end)
        row_ref[0, :, lo:hi] = cc.T[:CELLS, :]
        dec_ref[0, i] = jnp.exp(m - m_end)
        m = total + m_end
    m_ref[...] = m


def _gate_scan(raw, bias):
    B, S, _ = raw.shape
    rows = SCAN_CHUNKS * CHUNK
    nj = S // rows
    return pl.pallas_call(
        _gate_scan_kernel,
        grid=(B, nj),
        in_specs=[
            pl.BlockSpec((1, rows, GATE_W), lambda b, j: (b, j, 0)),
            pl.BlockSpec((1, rows, GATE_W), lambda b, j: (b, nj - 1 - j, 0)),
            pl.BlockSpec((1, GATE_W), lambda b, j: (0, 0)),
        ],
        out_specs=[
            pl.BlockSpec((1, rows, G_GROUPS * LANES), lambda b, j: (b, j, 0)),
            pl.BlockSpec((1, CELLS, rows), lambda b, j: (b, 0, j)),
            pl.BlockSpec((1, SCAN_CHUNKS, 1, LANES), lambda b, j: (b, j, 0, 0)),
        ],
        out_shape=[
            jax.ShapeDtypeStruct((B, S, G_GROUPS * LANES), F32),
            jax.ShapeDtypeStruct((B, CELLS, S), F32),
            jax.ShapeDtypeStruct((B, S // CHUNK, 1, LANES), F32),
        ],
        scratch_shapes=[pltpu.VMEM((1, LANES), F32)],
        compiler_params=_params("parallel", "arbitrary"),
        name="gate_scan",
    )(raw, raw, bias)


def _conv_kernel(main_ref, prev_ref, next_ref, w_ref, b_ref, lg_ref, lb_ref, out_ref, u_ref, *, tm):
    i = pl.program_id(1)
    halo = CONV_HALO
    span = tm + 2 * halo - SUBLANES

    def glu(blk):
        return blk[:, :CONV_DIM].astype(F32) * jax.nn.sigmoid(blk[:, CONV_DIM:].astype(F32))

    u_ref[0, 0:halo, :] = jnp.where(i > 0, glu(prev_ref[0]), 0.0)
    u_ref[0, halo:halo + tm, :] = glu(main_ref[0])
    u_ref[0, halo + tm:2 * halo + tm, :] = jnp.where(i < pl.num_programs(1) - 1, glu(next_ref[0]), 0.0)
    for s in range(1, SUBLANES):
        u_ref[s, 0:span, :] = u_ref[0, s:s + span, :]

    grp = CONV_ROWS // SUBLANES
    for r0 in range(0, tm, CONV_ROWS):
        acc = jnp.broadcast_to(b_ref[...].reshape(1, 1, CONV_DIM), (grp, SUBLANES, CONV_DIM))
        for j in range(CONV_WIDTH):
            off = r0 + halo - CONV_PAD + j
            lo = off - off % SUBLANES
            blk = u_ref[off % SUBLANES, lo:lo + CONV_ROWS, :].reshape(grp, SUBLANES, CONV_DIM)
            acc = acc + w_ref[j][None] * blk
        acc = acc.reshape(CONV_ROWS, CONV_DIM)
        mu = jnp.mean(acc, -1, keepdims=True)
        xc = acc - mu
        y = xc * lax.rsqrt(jnp.mean(xc * xc, -1, keepdims=True) + EPS) * lg_ref[...] + lb_ref[...]
        out_ref[0, r0:r0 + CONV_ROWS, :] = (y * jax.nn.sigmoid(y)).astype(BF16)


def _conv_branch(proj, w, b, lg, lb):
    B, S, _ = proj.shape
    tm = min(CONV_TM, S)
    halo = CONV_HALO
    per = tm // halo
    last = S // halo - 1
    return pl.pallas_call(
        functools.partial(_conv_kernel, tm=tm),
        grid=(B, S // tm),
        in_specs=[
            pl.BlockSpec((1, tm, 2 * CONV_DIM), lambda b, i: (b, i, 0)),
            pl.BlockSpec((1, halo, 2 * CONV_DIM), lambda b, i: (b, jnp.maximum(i * per - 1, 0), 0)),
            pl.BlockSpec((1, halo, 2 * CONV_DIM), lambda b, i: (b, jnp.minimum((i + 1) * per, last), 0)),
            pl.BlockSpec((CONV_WIDTH, SUBLANES, CONV_DIM), lambda b, i: (0, 0, 0)),
            pl.BlockSpec((1, CONV_DIM), lambda b, i: (0, 0)),
            pl.BlockSpec((1, CONV_DIM), lambda b, i: (0, 0)),
            pl.BlockSpec((1, CONV_DIM), lambda b, i: (0, 0)),
        ],
        out_specs=pl.BlockSpec((1, tm, CONV_DIM), lambda b, i: (b, i, 0)),
        out_shape=jax.ShapeDtypeStruct((B, S, CONV_DIM), BF16),
        scratch_shapes=[pltpu.VMEM((SUBLANES, tm + 2 * halo, CONV_DIM), F32)],
        compiler_params=_params("parallel", "parallel"),
        name="conv_branch",
    )(proj, proj, proj, w, b, lg, lb)


def _mlstm_cell(q, k, v, alpha, c_row, inter, floor, kscale, decay, mask, c_ref, n_ref, idx):
    scale = DK ** -0.5
    c_state = c_ref[idx]
    n_state = n_ref[idx]
    w = jnp.exp(jnp.where(mask, alpha + c_row, -jnp.inf))
    qk = lax.dot_general(q, k, (((1,), (1,)), ((), ())), preferred_element_type=F32)
    s = qk * scale * w
    q_c = jnp.dot(q, c_state.astype(BF16), preferred_element_type=F32) * scale
    wk = kscale * k.astype(F32)
    both = jnp.dot(jnp.concatenate([s.astype(BF16), wk.astype(BF16).T], 0), v, preferred_element_type=F32)
    num = inter * q_c + both[:CHUNK]
    q_n = jnp.sum(q.astype(F32) * n_state, -1, keepdims=True) * scale
    den = inter * q_n + jnp.sum(s, -1, keepdims=True)
    h = num * (1.0 / jnp.maximum(jnp.abs(den), floor))
    c_ref[idx] = decay * c_state + both[CHUNK:]
    n_ref[idx] = decay * n_state + jnp.sum(wk, 0, keepdims=True)
    return h


def _mlstm_kernel(qf_ref, kf_ref, vf_ref, qb_ref, kb_ref, vb_ref, tab_ref, row_ref, dec_ref,
                  hf_ref, hb_ref, c_ref, n_ref):
    @pl.when(pl.program_id(1) == 0)
    def _():
        c_ref[...] = jnp.zeros_like(c_ref)
        n_ref[...] = jnp.zeros_like(n_ref)

    r = lax.broadcasted_iota(jnp.int32, (CHUNK, CHUNK), 0)
    c = lax.broadcasted_iota(jnp.int32, (CHUNK, CHUNK), 1)
    dirs = ((qf_ref, kf_ref, vf_ref, hf_ref, c <= r), (qb_ref, kb_ref, vb_ref, hb_ref, c >= r))
    decay = dec_ref[0, 0]
    for d, (q_ref, k_ref, v_ref, h_ref, mask) in enumerate(dirs):
        for h in range(HEADS):
            cell = d * HEADS + h
            col = lambda grp: tab_ref[0, :, grp * LANES + cell:grp * LANES + cell + 1]
            out = _mlstm_cell(
                q_ref[0, :, h * DK:(h + 1) * DK], k_ref[0, :, h * DK:(h + 1) * DK],
                v_ref[0, :, h * DV:(h + 1) * DV],
                col(G_ALPHA), row_ref[0, cell:cell + 1, :], col(G_INTER), col(G_FLOOR), col(G_KSCALE),
                decay[:, cell:cell + 1], mask, c_ref, n_ref, cell)
            h_ref[0, :, h * DV:(h + 1) * DV] = out.astype(BF16)


def _mlstm(proj, tab, row_c, dec):
    B, S, _ = proj.shape
    nc = S // CHUNK
    fwd = lambda blk: (lambda b, c: (b, c, blk))
    bwd = lambda blk: (lambda b, c: (b, nc - 1 - c, blk))
    qkv = lambda im: [
        pl.BlockSpec((1, CHUNK, QK_DIM), im(COL_Q // QK_DIM)),
        pl.BlockSpec((1, CHUNK, QK_DIM), im(COL_K // QK_DIM)),
        pl.BlockSpec((1, CHUNK, V_DIM), im(COL_V // V_DIM)),
    ]
    return pl.pallas_call(
        _mlstm_kernel,
        grid=(B, nc),
        in_specs=qkv(fwd) + qkv(bwd) + [
            pl.BlockSpec((1, CHUNK, G_GROUPS * LANES), fwd(0)),
            pl.BlockSpec((1, CELLS, CHUNK), lambda b, c: (b, 0, c)),
            pl.BlockSpec((1, 1, 1, LANES), lambda b, c: (b, c, 0, 0)),
        ],
        out_specs=[
            pl.BlockSpec((1, CHUNK, V_DIM), fwd(0)),
            pl.BlockSpec((1, CHUNK, V_DIM), bwd(0)),
        ],
        out_shape=[jax.ShapeDtypeStruct((B, S, V_DIM), BF16)] * 2,
        scratch_shapes=[
            pltpu.VMEM((CELLS, DK, DV), F32),
            pltpu.VMEM((CELLS, 1, DK), F32),
        ],
        compiler_params=_params("parallel", "arbitrary"),
        name="mlstm",
    )(proj, proj, proj, proj, proj, proj, tab, row_c, dec)


def _merge_kernel(hf_ref, hb_ref, o_ref, mg_ref, u_ref, x_ref, hg_ref, wc_ref, wm_ref, wo_ref, out_ref):
    hs = hf_ref[0].astype(F32) + hb_ref[0].astype(F32)
    heads = []
    for h in range(HEADS):
        blk = hs[:, h * DV:(h + 1) * DV]
        heads.append(_rms(blk, hg_ref[:, h * DV:(h + 1) * DV]))
    hn = jnp.concatenate(heads, -1) * jax.nn.sigmoid(o_ref[0].astype(F32))
    y_mlstm = jnp.dot(hn.astype(BF16), wm_ref[...], preferred_element_type=F32)
    y_conv = jnp.dot(u_ref[0], wc_ref[...], preferred_element_type=F32)
    mg = mg_ref[0]
    mixed = (jax.nn.sigmoid(mg[:, :D_MODEL].astype(F32)) * y_conv
             + jax.nn.sigmoid(mg[:, D_MODEL:].astype(F32)) * y_mlstm)
    out_ref[0] = x_ref[0] + jnp.dot(mixed.astype(BF16), wo_ref[...], preferred_element_type=F32)


def _merge(hf, hb, proj, u, x, head_g, w_conv_out, w_mlstm_out, w_out):
    B, S, _ = x.shape
    tm = min(MERGE_TM, S)
    row = lambda width, blk: pl.BlockSpec((1, tm, width), lambda b, i: (b, i, blk))
    return pl.pallas_call(
        _merge_kernel,
        grid=(B, S // tm),
        in_specs=[
            row(V_DIM, 0), row(V_DIM, 0),
            row(V_DIM, COL_O // V_DIM),
            row(2 * D_MODEL, COL_MERGE // (2 * D_MODEL)),
            row(CONV_DIM, 0),
            row(D_MODEL, 0),
            _resident((1, V_DIM)),
            _resident((CONV_DIM, D_MODEL)),
            _resident((V_DIM, D_MODEL)),
            _resident((D_MODEL, D_MODEL)),
        ],
        out_specs=row(D_MODEL, 0),
        out_shape=jax.ShapeDtypeStruct((B, S, D_MODEL), F32),
        compiler_params=_params("parallel", "parallel"),
        name="merge",
    )(hf, hb, proj, proj, u, x, head_g, w_conv_out, w_mlstm_out, w_out)


def _ffn_kernel(x_ref, prev_ref, next_ref, g_ref, wg_ref, wv_ref, cw_ref, cb_ref, wd_ref, gf_ref,
                out_ref, xn_ref, *, tm, final_norm):
    i, j = pl.program_id(1), pl.program_id(2)
    halo = FFN_HALO

    @pl.when(j == 0)
    def _():
        g = g_ref[...]
        xn_ref[0:tm, :] = _rms(x_ref[0], g).astype(BF16)
        edge = jnp.concatenate([
            jnp.where(i > 0, _rms(prev_ref[0], g), 0.0),
            jnp.where(i < pl.num_programs(1) - 1, _rms(next_ref[0], g), 0.0)], 0)
        xn_ref[tm:tm + 2 * halo, :] = edge.astype(BF16)
        out_ref[0] = x_ref[0]

    valid = D_FF - j * FFN_TF
    up_ok = lax.broadcasted_iota(jnp.int32, wg_ref.shape, 1) < valid
    zero = jnp.zeros((), BF16)
    gate_all = jnp.dot(xn_ref[...], jnp.where(up_ok, wg_ref[...], zero), preferred_element_type=F32)
    val = jnp.dot(xn_ref[0:tm, :], jnp.where(up_ok, wv_ref[...], zero), preferred_element_type=F32)
    gate = gate_all[0:tm]
    before = gate_all[tm + halo - 1:tm + halo]
    after = gate_all[tm + halo:tm + halo + 1]
    row_id = lax.broadcasted_iota(jnp.int32, (SUBLANES, gate.shape[1]), 0)
    down = pltpu.roll(gate, 1, 0)
    down = jnp.concatenate([jnp.where(row_id == 0, before, down[0:SUBLANES]), down[SUBLANES:]], 0)
    up = pltpu.roll(gate, tm - 1, 0)
    up = jnp.concatenate([up[:tm - SUBLANES], jnp.where(row_id == SUBLANES - 1, after, up[tm - SUBLANES:])], 0)
    conv = cw_ref[0:1, :] * down + cw_ref[1:2, :] * gate + cw_ref[2:3, :] * up + cb_ref[...]
    hid = 0.5 * conv * (1.0 + lax.erf(conv * (2.0 ** -0.5))) * val
    wd = jnp.where(lax.broadcasted_iota(jnp.int32, wd_ref.shape, 0) < valid, wd_ref[...], zero)
    out_ref[0] += jnp.dot(hid.astype(BF16), wd, preferred_element_type=F32)

    if final_norm:
        @pl.when(j == pl.num_programs(2) - 1)
        def _():
            out_ref[0] = _rms(out_ref[0], gf_ref[...])


def _ffn(x, g, w_gate, w_val, conv_w, conv_b, w_down, layer, g_final, final_norm):
    B, S, _ = x.shape
    tm = min(FFN_TM, S)
    halo = FFN_HALO
    per = tm // halo
    last = S // halo - 1
    return pl.pallas_call(
        functools.partial(_ffn_kernel, tm=tm, final_norm=final_norm),
        grid=(B, S // tm, D_FF_PAD // FFN_TF),
        in_specs=[
            pl.BlockSpec((1, tm, D_MODEL), lambda b, i, j: (b, i, 0)),
            pl.BlockSpec((1, halo, D_MODEL), lambda b, i, j: (b, jnp.maximum(i * per - 1, 0), 0)),
            pl.BlockSpec((1, halo, D_MODEL), lambda b, i, j: (b, jnp.minimum((i + 1) * per, last), 0)),
            pl.BlockSpec((1, D_MODEL), lambda b, i, j: (0, 0)),
            pl.BlockSpec((None, D_MODEL, FFN_TF), lambda b, i, j: (layer, 0, j)),
            pl.BlockSpec((None, D_MODEL, FFN_TF), lambda b, i, j: (layer, 0, j)),
            pl.BlockSpec((SUBLANES, FFN_TF), lambda b, i, j: (0, j)),
            pl.BlockSpec((1, FFN_TF), lambda b, i, j: (0, j)),
            pl.BlockSpec((None, FFN_TF, D_MODEL), lambda b, i, j: (layer, j, 0)),
            pl.BlockSpec((1, D_MODEL), lambda b, i, j: (0, 0)),
        ],
        out_specs=pl.BlockSpec((1, tm, D_MODEL), lambda b, i, j: (b, i, 0)),
        out_shape=jax.ShapeDtypeStruct((B, S, D_MODEL), F32),
        scratch_shapes=[pltpu.VMEM((tm + 2 * halo, D_MODEL), BF16)],
        compiler_params=_params("parallel", "parallel", "arbitrary", vmem_limit=FFN_VMEM_LIMIT),
        name="ffn",
    )(x, x, x, g, w_gate, w_val, conv_w, conv_b, w_down, g_final)


def _pad_to(a, axis, size):
    pad = [(0, 0)] * a.ndim
    pad[axis] = (0, size - a.shape[axis])
    return jnp.pad(a, pad)


def _cell_gates(a):
    g = a.reshape(a.shape[:-1] + (4, HEADS))
    ig = g[..., 0::2, :].reshape(a.shape[:-1] + (CELLS,))
    lf = g[..., 1::2, :].reshape(a.shape[:-1] + (CELLS,))
    return jnp.concatenate([_pad_to(ig, -1 % a.ndim, LANES), _pad_to(lf, -1 % a.ndim, LANES)], -1)


def _stacked_weights(w_in, w_up, w_down):
    wt = jnp.swapaxes(w_in, 1, 2)
    return dict(
        w_main=jnp.where(lax.broadcasted_iota(jnp.int32, (1, D_MAIN, 1), 1) < GATE_OFF,
                         wt[:, :D_MAIN], wt[:, N_GATES:N_GATES + D_MAIN]).astype(BF16),
        w_gates=wt[:, GATE_OFF:GATE_OFF + N_GATES].astype(BF16),
        w_up_gate=w_up[:, :, :D_FF].astype(BF16),
        w_up_val=w_up[:, :, D_FF:].astype(BF16),
        w_down=w_down.astype(BF16),
    )


def _layer_params(l, norm_mix_g, w_gates, b_gates, conv_dw_w, conv_dw_b, conv_ln_g, conv_ln_b, w_conv_out,
                  mlstm_head_g, w_mlstm_out, w_out, norm_ffn_g, w_up, ffn_dw_w, ffn_dw_b, w_down):
    row = lambda a: a[l].reshape(1, -1)
    return dict(
        norm_mix_g=row(norm_mix_g),
        w_gate=_cell_gates(w_gates[l].T),
        b_gates=_cell_gates(row(b_gates)),
        conv_w=jnp.broadcast_to(conv_dw_w[l][:, None, :], (CONV_WIDTH, SUBLANES, CONV_DIM)),
        conv_b=row(conv_dw_b), ln_g=row(conv_ln_g), ln_b=row(conv_ln_b),
        w_conv_out=w_conv_out[l].astype(BF16),
        head_g=row(mlstm_head_g),
        w_mlstm_out=w_mlstm_out[l].astype(BF16),
        w_out=w_out[l].astype(BF16),
        norm_ffn_g=row(norm_ffn_g),
        ffn_w=_pad_to(_pad_to(ffn_dw_w[l], 0, SUBLANES), 1, D_FF_PAD),
        ffn_b=_pad_to(row(ffn_dw_b), 1, D_FF_PAD),
    )


def _trunk(x, layers, big, g_final):
    for l, p in enumerate(layers):
        proj, gates = _in_proj(x, p["norm_mix_g"], big["w_main"], l, p["w_gate"])
        tab, row_c, dec = _gate_scan(gates, p["b_gates"])
        u = _conv_branch(proj, p["conv_w"], p["conv_b"], p["ln_g"], p["ln_b"])
        hf, hb = _mlstm(proj, tab, row_c, dec)
        x = _merge(hf, hb, proj, u, x, p["head_g"], p["w_conv_out"], p["w_mlstm_out"], p["w_out"])
        x = _ffn(x, p["norm_ffn_g"], big["w_up_gate"], big["w_up_val"], p["ffn_w"], p["ffn_b"], big["w_down"], l,
                 g_final, final_norm=(l == len(layers) - 1))
    return x


def kernel(x_prompt, x_sample, norm_mix_g, w_in, b_gates, conv_dw_w, conv_dw_b, conv_ln_g, conv_ln_b,
           w_conv_out, mlstm_head_g, w_mlstm_out, w_out, norm_ffn_g, w_up, ffn_dw_w, ffn_dw_b, w_down,
           norm_final_g):
    depth = w_in.shape[0]
    big = _stacked_weights(w_in, w_up, w_down)
    layers = [
        _layer_params(l, norm_mix_g, big["w_gates"], b_gates, conv_dw_w, conv_dw_b, conv_ln_g, conv_ln_b, w_conv_out,
                      mlstm_head_g, w_mlstm_out, w_out, norm_ffn_g, w_up, ffn_dw_w, ffn_dw_b, w_down)
        for l in range(depth)
    ]
    g_final = norm_final_g.reshape(1, -1)
    return (_trunk(x_prompt, layers, big, g_final), _trunk(x_sample, layers, big, g_final))
```
